```python
import math
import jax, jax.numpy as jnp
from jax import lax
import numpy as np

D_MODEL = 1024
BATCH = 16
SEQ = 2048
DEPTH = 1

MIX_WIDTH = D_MODEL
FNET_WIDTH = MIX_WIDTH // 2
FNET_GROUPS = 4
FNET_GROUP_DIM = FNET_WIDTH // FNET_GROUPS
HEAD_DIM = 64
N_Q_HEADS = (MIX_WIDTH - FNET_WIDTH) // HEAD_DIM
N_KV_HEADS = 2
GQA_GROUP = N_Q_HEADS // N_KV_HEADS
ATTN_WIDTH = N_Q_HEADS * HEAD_DIM
KV_WIDTH = N_KV_HEADS * HEAD_DIM
IN_WIDTH = FNET_WIDTH + ATTN_WIDTH + 2 * KV_WIDTH
D_FF = 2752
GRID_W = 64
AXIS_DIM = HEAD_DIM // 2
ROPE_THETA = 10000.0
Q_BLOCK = 128
EPS = 1e-6

kernel_name = "hybrid_fnet_gqa_axial_macaron_encoder"


def rms_norm(x, g):
    xf = x.astype(jnp.float32)
    y = xf * lax.rsqrt(jnp.mean(xf * xf, axis=-1, keepdims=True) + EPS)
    return (y * g.astype(jnp.float32)).astype(x.dtype)


def swiglu(h, w_gate, w_up, w_down):
    return (jax.nn.silu(h @ w_gate) * (h @ w_up)) @ w_down


def axial_rope_tables(seq):
    rows = seq // GRID_W
    row_idx = jnp.repeat(jnp.arange(rows, dtype=jnp.float32), GRID_W)
    col_idx = jnp.tile(jnp.arange(GRID_W, dtype=jnp.float32), rows)
    inv_freq = ROPE_THETA ** (-jnp.arange(0, AXIS_DIM, 2, dtype=jnp.float32) / AXIS_DIM)
    ang_r = row_idx[:, None] * inv_freq[None, :]
    ang_c = col_idx[:, None] * inv_freq[None, :]
    return jnp.cos(ang_r), jnp.sin(ang_r), jnp.cos(ang_c), jnp.sin(ang_c)


def rotate(x, cos, sin):
    half = x.shape[-1] // 2
    x1, x2 = x[..., :half], x[..., half:]
    c = cos[None, :, None, :].astype(x.dtype)
    s = sin[None, :, None, :].astype(x.dtype)
    return jnp.concatenate([x1 * c - x2 * s, x1 * s + x2 * c], axis=-1)


def apply_axial_rope(x, tables):
    cr, sr, cc, sc = tables
    return jnp.concatenate([rotate(x[..., :AXIS_DIM], cr, sr),
                            rotate(x[..., AXIS_DIM:], cc, sc)], axis=-1)


def fourier_mixer(hf, fnet_w, fnet_b):
    b, s, _ = hf.shape
    g = hf.reshape(b, s, FNET_GROUPS, FNET_GROUP_DIM).astype(jnp.float32)
    mixed = jnp.real(jnp.fft.fft2(g, axes=(1, 3), norm="ortho")).astype(hf.dtype)
    out = jnp.einsum('bsgc,gcd->bsgd', mixed, fnet_w) + fnet_b[None, None]
    return out.reshape(b, s, FNET_WIDTH)


def gqa_attention(q, k, v, q_norm, k_norm):
    b, s, _ = q.shape
    q = rms_norm(q.reshape(b, s, N_Q_HEADS, HEAD_DIM), q_norm)
    k = rms_norm(k.reshape(b, s, N_KV_HEADS, HEAD_DIM), k_norm)
    v = v.reshape(b, s, N_KV_HEADS, HEAD_DIM)
    tables = axial_rope_tables(s)
    q = apply_axial_rope(q, tables)
    k = apply_axial_rope(k, tables)
    scale = 1.0 / math.sqrt(HEAD_DIM)
    n_blocks = s // Q_BLOCK
    qb = q.reshape(b, n_blocks, Q_BLOCK, N_KV_HEADS, GQA_GROUP, HEAD_DIM)
    qb = jnp.moveaxis(qb, 1, 0)

    def attend_block(q_blk):
        scores = jnp.einsum('bqhgd,bkhd->bhgqk', q_blk, k).astype(jnp.float32) * scale
        p = jax.nn.softmax(scores, axis=-1).astype(v.dtype)
        return jnp.einsum('bhgqk,bkhd->bqhgd', p, v)

    o = lax.map(attend_block, qb)
    o = jnp.moveaxis(o, 0, 1)
    return o.reshape(b, s, ATTN_WIDTH)


def setup_inputs(seed: int = 0) -> dict:
    key = jax.random.key(seed)
    ks = jax.random.split(key, 20)
    D, F = D_MODEL, D_FF
    nrm = lambda k, shape, fan_in: jax.random.normal(k, shape, jnp.float32) * fan_in ** -0.5
    gain = lambda k, n: 1.0 + 0.05 * jax.random.normal(k, (n,), jnp.float32)
    return {
        "x": jax.random.normal(ks[0], (BATCH, SEQ, D), jnp.float32),
        "ffn1_norm": gain(ks[1], D),
        "ffn1_w_gate": nrm(ks[2], (D, F), D),
        "ffn1_w_up": nrm(ks[3], (D, F), D),
        "ffn1_w_down": nrm(ks[4], (F, D), F),
        "mix_norm": gain(ks[5], D),
        "w_in": nrm(ks[6], (D, IN_WIDTH), D),
        "fnet_w": nrm(ks[7], (FNET_GROUPS, FNET_GROUP_DIM, FNET_GROUP_DIM), FNET_GROUP_DIM),
        "fnet_b": 0.02 * jax.random.normal(ks[8], (FNET_GROUPS, FNET_GROUP_DIM), jnp.float32),
        "q_norm": gain(ks[9], HEAD_DIM),
        "k_norm": gain(ks[10], HEAD_DIM),
        "w_out": nrm(ks[11], (MIX_WIDTH, D), MIX_WIDTH),
        "ffn2_norm": gain(ks[12], D),
        "ffn2_w_gate": nrm(ks[13], (D, F), D),
        "ffn2_w_up": nrm(ks[14], (D, F), D),
        "ffn2_w_down": nrm(ks[15], (F, D), F),
        "final_norm": gain(ks[16], D),
    }


def reference(x, ffn1_norm, ffn1_w_gate, ffn1_w_up, ffn1_w_down, mix_norm, w_in,
              fnet_w, fnet_b, q_norm, k_norm, w_out, ffn2_norm, ffn2_w_gate,
              ffn2_w_up, ffn2_w_down, final_norm):
    for _ in range(DEPTH):
        x = x + 0.5 * swiglu(rms_norm(x, ffn1_norm), ffn1_w_gate, ffn1_w_up, ffn1_w_down)
        h = rms_norm(x, mix_norm)
        u = h @ w_in
        o0 = FNET_WIDTH
        o1 = o0 + ATTN_WIDTH
        o2 = o1 + KV_WIDTH
        f_out = fourier_mixer(u[..., :o0], fnet_w, fnet_b)
        a_out = gqa_attention(u[..., o0:o1], u[..., o1:o2], u[..., o2:], q_norm, k_norm)
        x = x + jnp.concatenate([f_out, a_out], axis=-1) @ w_out
        x = x + 0.5 * swiglu(rms_norm(x, ffn2_norm), ffn2_w_gate, ffn2_w_up, ffn2_w_down)
    return rms_norm(x, final_norm)
```

```python
import functools
import math

import jax
import jax.numpy as jnp
from jax import lax
from jax.experimental import pallas as pl
from jax.experimental.pallas import tpu as pltpu

D_MODEL = 1024
D_FF = 2752
FNET_WIDTH = 512
FNET_GROUPS = 4
FNET_GROUP_DIM = 128
HEAD_DIM = 64
N_Q_HEADS = 8
N_KV_HEADS = 2
GQA_GROUP = N_Q_HEADS // N_KV_HEADS
ATTN_WIDTH = N_Q_HEADS * HEAD_DIM
KV_WIDTH = N_KV_HEADS * HEAD_DIM
QK_WIDTH = ATTN_WIDTH + KV_WIDTH
IN_WIDTH = FNET_WIDTH + ATTN_WIDTH + 2 * KV_WIDTH
GRID_W = 64
AXIS_DIM = HEAD_DIM // 2
ROPE_THETA = 10000.0
EPS = 1e-6

LANES = 128
MXU_DIM = 256
VMEM_LIMIT_BYTES = 60000 * 1024

D_FF_PAD = ((D_FF + MXU_DIM - 1) // MXU_DIM) * MXU_DIM

TOKEN_TILE = 256
ATTN_Q_TILE = 256

BF16 = jnp.bfloat16
F32 = jnp.float32


def _rms_norm(x, gain):
    return x * lax.rsqrt(jnp.mean(x * x, axis=-1, keepdims=True) + EPS) * gain


def _dot(a, b):
    return jnp.dot(a, b, preferred_element_type=F32)


def _swiglu_half_step(x, gain, wg_ref, wu_ref, wd_ref):
    xn = _rms_norm(x, gain).astype(BF16)
    g = _dot(xn, wg_ref[...])
    u = _dot(xn, wu_ref[...])
    a = (g * jax.nn.sigmoid(g) * u).astype(BF16)
    return x + 0.5 * _dot(a, wd_ref[...])


def _split_hi_lo(x):
    hi = x.astype(BF16)
    lo = (x - hi.astype(F32)).astype(BF16)
    return hi, lo


def _ffn1_inproj_kernel(x_ref, g1_ref, wg_ref, wu_ref, wd_ref, gm_ref, win_ref, gqk_ref,
                        cos_ref, sin_ref, seg_ref,
                        x1_ref, uf_ref, q_ref, k_ref, v_ref):
    x1 = _swiglu_half_step(x_ref[...], g1_ref[...], wg_ref, wu_ref, wd_ref)
    x1_ref[...] = x1

    h = _rms_norm(x1, gm_ref[...]).astype(BF16)
    u = _dot(h, win_ref[...])
    uf_ref[...] = u[:, :FNET_WIDTH].astype(BF16)
    v_ref[...] = u[:, FNET_WIDTH + QK_WIDTH:].astype(BF16)

    seg = seg_ref[...]
    cos_t = cos_ref[...]
    sin_t = sin_ref[...]
    lane = lax.broadcasted_iota(jnp.int32, (1, LANES), 1)
    upper_half = (lane & (AXIS_DIM // 2)) != 0
    outs = []
    for c in range(QK_WIDTH // LANES):
        lo_col = FNET_WIDTH + c * LANES
        z = u[:, lo_col:lo_col + LANES]
        hi, lo = _split_hi_lo(z * z)
        mean_sq = (_dot(hi, seg) + _dot(lo, seg)) * (1.0 / HEAD_DIM)
        zn = z * lax.rsqrt(mean_sq + EPS) * gqk_ref[:, c * LANES:(c + 1) * LANES]
        partner = jnp.where(upper_half,
                            pltpu.roll(zn, AXIS_DIM // 2, axis=1),
                            pltpu.roll(zn, LANES - AXIS_DIM // 2, axis=1))
        outs.append((zn * cos_t + partner * sin_t).astype(BF16))
    q_ref[...] = jnp.concatenate(outs[:ATTN_WIDTH // LANES], axis=-1)
    k_ref[...] = outs[ATTN_WIDTH // LANES]


def _fourier_kernel(uf_ref, dft_ref, cc_ref, sc_ref, fw_ref, fb_ref, out_ref, y_ref):
    seq = uf_ref.shape[0]
    cc_hi, cc_lo = _split_hi_lo(cc_ref[...])
    sc_hi, sc_lo = _split_hi_lo(sc_ref[...])
    for g in range(FNET_GROUPS):
        w_hi, w_lo = _split_hi_lo(fw_ref[g])
        wc = (_dot(cc_hi, w_hi) + _dot(cc_hi, w_lo) + _dot(cc_lo, w_hi)).astype(BF16)
        ws = (_dot(sc_hi, w_hi) + _dot(sc_hi, w_lo) + _dot(sc_lo, w_hi)).astype(BF16)
        ug = uf_ref[:, g * FNET_GROUP_DIM:(g + 1) * FNET_GROUP_DIM]
        y_ref[:seq, g * FNET_GROUP_DIM:(g + 1) * FNET_GROUP_DIM] = _dot(ug, wc).astype(BF16)
        y_ref[seq:, g * FNET_GROUP_DIM:(g + 1) * FNET_GROUP_DIM] = _dot(ug, ws).astype(BF16)
    out_ref[...] = (_dot(dft_ref[...], y_ref[...]) + fb_ref[...]).astype(BF16)


def _attention_kernel(q_ref, k_ref, v_ref, o_ref):
    outs = []
    for kv in range(N_KV_HEADS):
        k = k_ref[:, kv * HEAD_DIM:(kv + 1) * HEAD_DIM]
        v = v_ref[:, kv * HEAD_DIM:(kv + 1) * HEAD_DIM]
        for j in range(GQA_GROUP):
            h = kv * GQA_GROUP + j
            q = q_ref[:, h * HEAD_DIM:(h + 1) * HEAD_DIM]
            s = lax.dot_general(q, k, (((1,), (1,)), ((), ())), preferred_element_type=F32)
            p = jnp.exp(s - jnp.max(s, axis=-1, keepdims=True))
            denom = jnp.sum(p, axis=-1, keepdims=True)
            outs.append(_dot(p.astype(BF16), v) / denom)
    o_ref[...] = jnp.concatenate(outs, axis=-1).astype(BF16)


def _outproj_ffn2_kernel(x1_ref, f_ref, a_ref, wof_ref, woa_ref, g2_ref, wg_ref, wu_ref, wd_ref,
                         gf_ref, y_ref):
    x2 = x1_ref[...] + _dot(f_ref[...], wof_ref[...]) + _dot(a_ref[...], woa_ref[...])
    x3 = _swiglu_half_step(x2, g2_ref[...], wg_ref, wu_ref, wd_ref)
    y_ref[...] = _rms_norm(x3, gf_ref[...])


def _resident(shape):
    return pl.BlockSpec(shape, lambda *_: (0,) * len(shape), pipeline_mode=pl.Buffered(1))


def _rope_tables(seq):
    rows = seq // GRID_W
    row_idx = jnp.repeat(jnp.arange(rows, dtype=F32), GRID_W)
    col_idx = jnp.tile(jnp.arange(GRID_W, dtype=F32), rows)
    inv_freq = ROPE_THETA ** (-jnp.arange(0, AXIS_DIM, 2, dtype=F32) / AXIS_DIM)
    ang_r = row_idx[:, None] * inv_freq[None, :]
    ang_c = col_idx[:, None] * inv_freq[None, :]
    cos_h = jnp.concatenate([jnp.cos(ang_r)] * 2 + [jnp.cos(ang_c)] * 2, axis=-1)
    sin_h = jnp.concatenate([-jnp.sin(ang_r), jnp.sin(ang_r), -jnp.sin(ang_c), jnp.sin(ang_c)], axis=-1)
    reps = LANES // HEAD_DIM
    return jnp.tile(cos_h, (1, reps)), jnp.tile(sin_h, (1, reps))


def _dft_cos_sin(n):
    idx = jnp.arange(n, dtype=jnp.int32)
    ang = ((idx[:, None] * idx[None, :]) % n).astype(F32) * (2.0 * math.pi / n)
    return jnp.cos(ang), jnp.sin(ang)


def _pad_ff_cols(w):
    return jnp.pad(w, ((0, 0), (0, D_FF_PAD - D_FF))).astype(BF16)


def _pad_ff_rows(w):
    return jnp.pad(w, ((0, D_FF_PAD - D_FF), (0, 0))).astype(BF16)


def _row(v):
    return v.reshape(1, -1).astype(F32)


def kernel(x, ffn1_norm, ffn1_w_gate, ffn1_w_up, ffn1_w_down, mix_norm, w_in, fnet_w, fnet_b, q_norm,
           k_norm, w_out, ffn2_norm, ffn2_w_gate, ffn2_w_up, ffn2_w_down, final_norm):
    batch, seq, d = x.shape
    assert d == D_MODEL and seq % TOKEN_TILE == 0 and seq % ATTN_Q_TILE == 0 and seq % GRID_W == 0
    n_tok = batch * seq
    tiles_per_seq = seq // TOKEN_TILE
    x2d = x.reshape(n_tok, d)

    params = pltpu.CompilerParams(dimension_semantics=("arbitrary",), vmem_limit_bytes=VMEM_LIMIT_BYTES)
    tok = lambda width: pl.BlockSpec((TOKEN_TILE, width), lambda i: (i, 0))

    cos_t, sin_t = _rope_tables(seq)
    scale = 1.0 / math.sqrt(HEAD_DIM)
    gain_qk = jnp.concatenate([jnp.tile(q_norm.astype(F32) * scale, N_Q_HEADS),
                               jnp.tile(k_norm.astype(F32), N_KV_HEADS)]).reshape(1, QK_WIDTH)
    head_of_lane = jnp.arange(LANES) // HEAD_DIM
    seg_ones = (head_of_lane[:, None] == head_of_lane[None, :]).astype(BF16)
    rope_spec = pl.BlockSpec((TOKEN_TILE, LANES), lambda i: (i % tiles_per_seq, 0))

    x1, uf, q, k, v = pl.pallas_call(
        _ffn1_inproj_kernel,
        grid=(n_tok // TOKEN_TILE,),
        in_specs=[tok(d), _resident((1, d)),
                  _resident((d, D_FF_PAD)), _resident((d, D_FF_PAD)), _resident((D_FF_PAD, d)),
                  _resident((1, d)), _resident((d, IN_WIDTH)), _resident((1, QK_WIDTH)),
                  rope_spec, rope_spec, _resident((LANES, LANES))],
        out_specs=[tok(d), tok(FNET_WIDTH), tok(ATTN_WIDTH), tok(KV_WIDTH), tok(KV_WIDTH)],
        out_shape=[jax.ShapeDtypeStruct((n_tok, d), F32),
                   jax.ShapeDtypeStruct((n_tok, FNET_WIDTH), BF16),
                   jax.ShapeDtypeStruct((n_tok, ATTN_WIDTH), BF16),
                   jax.ShapeDtypeStruct((n_tok, KV_WIDTH), BF16),
                   jax.ShapeDtypeStruct((n_tok, KV_WIDTH), BF16)],
        compiler_params=params,
        name="ffn1_inproj",
    )(x2d, _row(ffn1_norm), _pad_ff_cols(ffn1_w_gate), _pad_ff_cols(ffn1_w_up), _pad_ff_rows(ffn1_w_down),
      _row(mix_norm), w_in.astype(BF16), gain_qk, cos_t, sin_t, seg_ones)

    cos_s, sin_s = _dft_cos_sin(seq)
    dft = jnp.concatenate([cos_s, -sin_s], axis=1).astype(BF16)
    cos_c, sin_c = _dft_cos_sin(FNET_GROUP_DIM)
    ortho = 1.0 / math.sqrt(seq * FNET_GROUP_DIM)
    f_out = pl.pallas_call(
        _fourier_kernel,
        grid=(batch,),
        in_specs=[pl.BlockSpec((None, seq, FNET_WIDTH), lambda b: (b, 0, 0)),
                  _resident((seq, 2 * seq)),
                  _resident((FNET_GROUP_DIM, FNET_GROUP_DIM)), _resident((FNET_GROUP_DIM, FNET_GROUP_DIM)),
                  _resident((FNET_GROUPS, FNET_GROUP_DIM, FNET_GROUP_DIM)), _resident((1, FNET_WIDTH))],
        out_specs=pl.BlockSpec((None, seq, FNET_WIDTH), lambda b: (b, 0, 0)),
        out_shape=jax.ShapeDtypeStruct((batch, seq, FNET_WIDTH), BF16),
        scratch_shapes=[pltpu.VMEM((2 * seq, FNET_WIDTH), BF16)],
        compiler_params=params,
        name="fourier",
    )(uf.reshape(batch, seq, FNET_WIDTH), dft, cos_c * ortho, sin_c * ortho, fnet_w.astype(F32),
      fnet_b.reshape(1, FNET_WIDTH).astype(F32))

    a_out = pl.pallas_call(
        _attention_kernel,
        grid=(batch, seq // ATTN_Q_TILE),
        in_specs=[pl.BlockSpec((None, ATTN_Q_TILE, ATTN_WIDTH), lambda b, i: (b, i, 0)),
                  pl.BlockSpec((None, seq, KV_WIDTH), lambda b, i: (b, 0, 0)),
                  pl.BlockSpec((None, seq, KV_WIDTH), lambda b, i: (b, 0, 0))],
        out_specs=pl.BlockSpec((None, ATTN_Q_TILE, ATTN_WIDTH), lambda b, i: (b, i, 0)),
        out_shape=jax.ShapeDtypeStruct((batch, seq, ATTN_WIDTH), BF16),
        compiler_params=pltpu.CompilerParams(dimension_semantics=("arbitrary", "arbitrary"),
                                             vmem_limit_bytes=VMEM_LIMIT_BYTES),
        name="attention",
    )(q.reshape(batch, seq, ATTN_WIDTH), k.reshape(batch, seq, KV_WIDTH), v.reshape(batch, seq, KV_WIDTH))

    w_out_bf = w_out.astype(BF16)
    y = pl.pallas_call(
        _outproj_ffn2_kernel,
        grid=(n_tok // TOKEN_TILE,),
        in_specs=[tok(d), tok(FNET_WIDTH), tok(ATTN_WIDTH),
                  _resident((FNET_WIDTH, d)), _resident((ATTN_WIDTH, d)), _resident((1, d)),
                  _resident((d, D_FF_PAD)), _resident((d, D_FF_PAD)), _resident((D_FF_PAD, d)),
                  _resident((1, d))],
        out_specs=tok(d),
        out_shape=jax.ShapeDtypeStruct((n_tok, d), F32),
        compiler_params=params,
        name="outproj_ffn2",
    )(x1, f_out.reshape(n_tok, FNET_WIDTH), a_out.reshape(n_tok, ATTN_WIDTH),
      w_out_bf[:FNET_WIDTH], w_out_bf[FNET_WIDTH:], _row(ffn2_norm),
      _pad_ff_cols(ffn2_w_gate), _pad_ff_cols(ffn2_w_up), _pad_ff_rows(ffn2_w_down), _row(final_norm))
    return y.reshape(batch, seq, d)
```

```python
import functools
import math

import jax
import jax.numpy as jnp
from jax import lax
from jax.experimental import pallas as pl
from jax.experimental.pallas import tpu as pltpu

D_MODEL = 1024
D_FF = 2752
FNET_WIDTH = 512
FNET_GROUPS = 4
FNET_GROUP_DIM = 128
HEAD_DIM = 64
N_Q_HEADS = 8
N_KV_HEADS = 2
GQA_GROUP = N_Q_HEADS // N_KV_HEADS
ATTN_WIDTH = N_Q_HEADS * HEAD_DIM
KV_WIDTH = N_KV_HEADS * HEAD_DIM
QK_WIDTH = ATTN_WIDTH + KV_WIDTH
IN_WIDTH = FNET_WIDTH + ATTN_WIDTH + 2 * KV_WIDTH
GRID_W = 64
AXIS_DIM = HEAD_DIM // 2
ROPE_THETA = 10000.0
EPS = 1e-6

LANES = 128
MXU_DIM = 256
VMEM_LIMIT_BYTES = 60000 * 1024

D_FF_PAD = ((D_FF + MXU_DIM - 1) // MXU_DIM) * MXU_DIM

TOKEN_TILE = 512
SUB_TILE = 256
ATTN_Q_TILE = 256

BF16 = jnp.bfloat16
F32 = jnp.float32


def _rms_norm(x, gain):
    return x * lax.rsqrt(jnp.mean(x * x, axis=-1, keepdims=True) + EPS) * gain


def _dot(a, b):
    return jnp.dot(a, b, preferred_element_type=F32)


def _swiglu_half_step(x, gain, wg_ref, wu_ref, wd_ref):
    xn = _rms_norm(x, gain).astype(BF16)
    g = _dot(xn, wg_ref[...])
    u = _dot(xn, wu_ref[...])
    a = (g * jax.nn.sigmoid(g) * u).astype(BF16)
    return x + 0.5 * _dot(a, wd_ref[...])


def _split_hi_lo(x):
    hi = x.astype(BF16)
    lo = (x - hi.astype(F32)).astype(BF16)
    return hi, lo


def _ffn1_inproj_kernel(x_ref, g1_ref, wg_ref, wu_ref, wd_ref, gm_ref, win_ref, gqk_ref,
                        cos_ref, sin_ref, seg_ref,
                        x1_ref, uf_ref, q_ref, k_ref, v_ref):
    seg = seg_ref[...]
    lane = lax.broadcasted_iota(jnp.int32, (1, LANES), 1)
    upper_half = (lane & (AXIS_DIM // 2)) != 0
    for r in range(TOKEN_TILE // SUB_TILE):
        rows = slice(r * SUB_TILE, (r + 1) * SUB_TILE)
        x1 = _swiglu_half_step(x_ref[rows, :], g1_ref[...], wg_ref, wu_ref, wd_ref)
        x1_ref[rows, :] = x1

        h = _rms_norm(x1, gm_ref[...]).astype(BF16)
        u = _dot(h, win_ref[...])
        uf_ref[rows, :] = u[:, :FNET_WIDTH].astype(BF16)
        v_ref[rows, :] = u[:, FNET_WIDTH + QK_WIDTH:].astype(BF16)

        cos_t = cos_ref[rows, :]
        sin_t = sin_ref[rows, :]
        for c in range(QK_WIDTH // LANES):
            lo_col = FNET_WIDTH + c * LANES
            z = u[:, lo_col:lo_col + LANES]
            hi, lo = _split_hi_lo(z * z)
            mean_sq = (_dot(hi, seg) + _dot(lo, seg)) * (1.0 / HEAD_DIM)
            zn = z * lax.rsqrt(mean_sq + EPS) * gqk_ref[:, c * LANES:(c + 1) * LANES]
            partner = jnp.where(upper_half,
                                pltpu.roll(zn, AXIS_DIM // 2, axis=1),
                                pltpu.roll(zn, LANES - AXIS_DIM // 2, axis=1))
            out = (zn * cos_t + partner * sin_t).astype(BF16)
            if c < ATTN_WIDTH // LANES:
                q_ref[rows, c * LANES:(c + 1) * LANES] = out
            else:
                k_ref[rows, :] = out


def _fourier_kernel(uf_ref, dft_ref, cc_ref, sc_ref, fw_ref, fb_ref, out_ref, y_ref):
    seq = uf_ref.shape[0]
    cc_hi, cc_lo = _split_hi_lo(cc_ref[...])
    sc_hi, sc_lo = _split_hi_lo(sc_ref[...])
    for g in range(FNET_GROUPS):
        w_hi, w_lo = _split_hi_lo(fw_ref[g])
        wc = (_dot(cc_hi, w_hi) + _dot(cc_hi, w_lo) + _dot(cc_lo, w_hi)).astype(BF16)
        ws = (_dot(sc_hi, w_hi) + _dot(sc_hi, w_lo) + _dot(sc_lo, w_hi)).astype(BF16)
        ug = uf_ref[:, g * FNET_GROUP_DIM:(g + 1) * FNET_GROUP_DIM]
        y_ref[:seq, g * FNET_GROUP_DIM:(g + 1) * FNET_GROUP_DIM] = _dot(ug, wc).astype(BF16)
        y_ref[seq:, g * FNET_GROUP_DIM:(g + 1) * FNET_GROUP_DIM] = _dot(ug, ws).astype(BF16)
    out_ref[...] = (_dot(dft_ref[...], y_ref[...]) + fb_ref[...]).astype(BF16)


def _attention_kernel(q_ref, k_ref, v_ref, o_ref):
    seq = k_ref.shape[0]
    ones_cols = jnp.ones((seq, HEAD_DIM), BF16)
    outs = []
    for kv in range(N_KV_HEADS):
        k = k_ref[:, kv * HEAD_DIM:(kv + 1) * HEAD_DIM]
        v1 = jnp.concatenate([v_ref[:, kv * HEAD_DIM:(kv + 1) * HEAD_DIM], ones_cols], axis=-1)
        for j in range(GQA_GROUP):
            h = kv * GQA_GROUP + j
            q = q_ref[:, h * HEAD_DIM:(h + 1) * HEAD_DIM]
            s = lax.dot_general(q, k, (((1,), (1,)), ((), ())), preferred_element_type=F32)
            p = jnp.exp2(s - jnp.max(s, axis=-1, keepdims=True)).astype(BF16)
            o = _dot(p, v1)
            outs.append(o[:, :HEAD_DIM] / o[:, HEAD_DIM:HEAD_DIM + 1])
    o_ref[...] = jnp.concatenate(outs, axis=-1).astype(BF16)


def _outproj_ffn2_kernel(x1_ref, f_ref, a_ref, wof_ref, woa_ref, g2_ref, wg_ref, wu_ref, wd_ref,
                         gf_ref, y_ref):
    for r in range(TOKEN_TILE // SUB_TILE):
        rows = slice(r * SUB_TILE, (r + 1) * SUB_TILE)
        x2 = x1_ref[rows, :] + _dot(f_ref[rows, :], wof_ref[...]) + _dot(a_ref[rows, :], woa_ref[...])
        x3 = _swiglu_half_step(x2, g2_ref[...], wg_ref, wu_ref, wd_ref)
        y_ref[rows, :] = _rms_norm(x3, gf_ref[...])


def _resident(shape):
    return pl.BlockSpec(shape, lambda *_: (0,) * len(shape), pipeline_mode=pl.Buffered(1))


def _rope_tables(seq):
    rows = seq // GRID_W
    row_idx = jnp.repeat(jnp.arange(rows, dtype=F32), GRID_W)
    col_idx = jnp.tile(jnp.arange(GRID_W, dtype=F32), rows)
    inv_freq = ROPE_THETA ** (-jnp.arange(0, AXIS_DIM, 2, dtype=F32) / AXIS_DIM)
    ang_r = row_idx[:, None] * inv_freq[None, :]
    ang_c = col_idx[:, None] * inv_freq[None, :]
    cos_h = jnp.concatenate([jnp.cos(ang_r)] * 2 + [jnp.cos(ang_c)] * 2, axis=-1)
    sin_h = jnp.concatenate([-jnp.sin(ang_r), jnp.sin(ang_r), -jnp.sin(ang_c), jnp.sin(ang_c)], axis=-1)
    reps = LANES // HEAD_DIM
    return jnp.tile(cos_h, (1, reps)), jnp.tile(sin_h, (1, reps))


def _dft_cos_sin(n):
    idx = jnp.arange(n, dtype=jnp.int32)
    ang = ((idx[:, None] * idx[None, :]) % n).astype(F32) * (2.0 * math.pi / n)
    return jnp.cos(ang), jnp.sin(ang)


def _pad_ff_cols(w):
    return jnp.pad(w, ((0, 0), (0, D_FF_PAD - D_FF))).astype(BF16)


def _pad_ff_rows(w):
    return jnp.pad(w, ((0, D_FF_PAD - D_FF), (0, 0))).astype(BF16)


def _row(v):
    return v.reshape(1, -1).astype(F32)


def kernel(x, ffn1_norm, ffn1_w_gate, ffn1_w_up, ffn1_w_down, mix_norm, w_in, fnet_w, fnet_b, q_norm,
           k_norm, w_out, ffn2_norm, ffn2_w_gate, ffn2_w_up, ffn2_w_down, final_norm):
    batch, seq, d = x.shape
    assert d == D_MODEL and seq % TOKEN_TILE == 0 and seq % ATTN_Q_TILE == 0 and seq % GRID_W == 0
    n_tok = batch * seq
    tiles_per_seq = seq // TOKEN_TILE
    x2d = x.reshape(n_tok, d)

    params = pltpu.CompilerParams(dimension_semantics=("arbitrary",), vmem_limit_bytes=VMEM_LIMIT_BYTES)
    tok = lambda width: pl.BlockSpec((TOKEN_TILE, width), lambda i: (i, 0))

    cos_t, sin_t = _rope_tables(seq)
    scale = math.log2(math.e) / math.sqrt(HEAD_DIM)
    gain_qk = jnp.concatenate([jnp.tile(q_norm.astype(F32) * scale, N_Q_HEADS),
                               jnp.tile(k_norm.astype(F32), N_KV_HEADS)]).reshape(1, QK_WIDTH)
    head_of_lane = jnp.arange(LANES) // HEAD_DIM
    seg_ones = (head_of_lane[:, None] == head_of_lane[None, :]).astype(BF16)
    rope_spec = pl.BlockSpec((TOKEN_TILE, LANES), lambda i: (i % tiles_per_seq, 0))

    x1, uf, q, k, v = pl.pallas_call(
        _ffn1_inproj_kernel,
        grid=(n_tok // TOKEN_TILE,),
        in_specs=[tok(d), _resident((1, d)),
                  _resident((d, D_FF_PAD)), _resident((d, D_FF_PAD)), _resident((D_FF_PAD, d)),
                  _resident((1, d)), _resident((d, IN_WIDTH)), _resident((1, QK_WIDTH)),
                  rope_spec, rope_spec, _resident((LANES, LANES))],
        out_specs=[tok(d), tok(FNET_WIDTH), tok(ATTN_WIDTH), tok(KV_WIDTH), tok(KV_WIDTH)],
        out_shape=[jax.ShapeDtypeStruct((n_tok, d), F32),
                   jax.ShapeDtypeStruct((n_tok, FNET_WIDTH), BF16),
                   jax.ShapeDtypeStruct((n_tok, ATTN_WIDTH), BF16),
                   jax.ShapeDtypeStruct((n_tok, KV_WIDTH), BF16),
                   jax.ShapeDtypeStruct((n_tok, KV_WIDTH), BF16)],
        compiler_params=params,
        name="ffn1_inproj",
    )(x2d, _row(ffn1_norm), _pad_ff_cols(ffn1_w_gate), _pad_ff_cols(ffn1_w_up), _pad_ff_rows(ffn1_w_down),
      _row(mix_norm), w_in.astype(BF16), gain_qk, cos_t, sin_t, seg_ones)

    cos_s, sin_s = _dft_cos_sin(seq)
    dft = jnp.concatenate([cos_s, -sin_s], axis=1).astype(BF16)
    cos_c, sin_c = _dft_cos_sin(FNET_GROUP_DIM)
    ortho = 1.0 / math.sqrt(seq * FNET_GROUP_DIM)
    f_out = pl.pallas_call(
        _fourier_kernel,
        grid=(batch,),
        in_specs=[pl.BlockSpec((None, seq, FNET_WIDTH), lambda b: (b, 0, 0)),
                  _resident((seq, 2 * seq)),
                  _resident((FNET_GROUP_DIM, FNET_GROUP_DIM)), _resident((FNET_GROUP_DIM, FNET_GROUP_DIM)),
                  _resident((FNET_GROUPS, FNET_GROUP_DIM, FNET_GROUP_DIM)), _resident((1, FNET_WIDTH))],
        out_specs=pl.BlockSpec((None, seq, FNET_WIDTH), lambda b: (b, 0, 0)),
        out_shape=jax.ShapeDtypeStruct((batch, seq, FNET_WIDTH), BF16),
        scratch_shapes=[pltpu.VMEM((2 * seq, FNET_WIDTH), BF16)],
        compiler_params=params,
        name="fourier",
    )(uf.reshape(batch, seq, FNET_WIDTH), dft, cos_c * ortho, sin_c * ortho, fnet_w.astype(F32),
      fnet_b.reshape(1, FNET_WIDTH).astype(F32))

    a_out = pl.pallas_call(
        _attention_kernel,
        grid=(batch, seq // ATTN_Q_TILE),
        in_specs=[pl.BlockSpec((None, ATTN_Q_TILE, ATTN_WIDTH), lambda b, i: (b, i, 0)),
                  pl.BlockSpec((None, seq, KV_WIDTH), lambda b, i: (b, 0, 0)),
                  pl.BlockSpec((None, seq, KV_WIDTH), lambda b, i: (b, 0, 0))],
        out_specs=pl.BlockSpec((None, ATTN_Q_TILE, ATTN_WIDTH), lambda b, i: (b, i, 0)),
        out_shape=jax.ShapeDtypeStruct((batch, seq, ATTN_WIDTH), BF16),
        compiler_params=pltpu.CompilerParams(dimension_semantics=("arbitrary", "arbitrary"),
                                             vmem_limit_bytes=VMEM_LIMIT_BYTES),
        name="attention",
    )(q.reshape(batch, seq, ATTN_WIDTH), k.reshape(batch, seq, KV_WIDTH), v.reshape(batch, seq, KV_WIDTH))

    w_out_bf = w_out.astype(BF16)
    y = pl.pallas_call(
        _outproj_ffn2_kernel,
        grid=(n_tok // TOKEN_TILE,),
        in_specs=[tok(d), tok(FNET_WIDTH), tok(ATTN_WIDTH),
                  _resident((FNET_WIDTH, d)), _resident((ATTN_WIDTH, d)), _resident((1, d)),
                  _resident((d, D_FF_PAD)), _resident((d, D_FF_PAD)), _resident((D_FF_PAD, d)),
                  _resident((1, d))],
        out_specs=tok(d),
        out_shape=jax.ShapeDtypeStruct((n_tok, d), F32),
        compiler_params=params,
        name="outproj_ffn2",
    )(x1, f_out.reshape(n_tok, FNET_WIDTH), a_out.reshape(n_tok, ATTN_WIDTH),
      w_out_bf[:FNET_WIDTH], w_out_bf[FNET_WIDTH:], _row(ffn2_norm),
      _pad_ff_cols(ffn2_w_gate), _pad_ff_cols(ffn2_w_up), _pad_ff_rows(ffn2_w_down), _row(final_norm))
    return y.reshape(batch, seq, d)
```

```python
import functools
import math

import jax
import jax.numpy as jnp
from jax import lax
from jax.experimental import pallas as pl
from jax.experimental.pallas import tpu as pltpu

D_MODEL = 1024
D_FF = 2752
FNET_WIDTH = 512
FNET_GROUPS = 4
FNET_GROUP_DIM = 128
HEAD_DIM = 64
N_Q_HEADS = 8
N_KV_HEADS = 2
GQA_GROUP = N_Q_HEADS // N_KV_HEADS
ATTN_WIDTH = N_Q_HEADS * HEAD_DIM
KV_WIDTH = N_KV_HEADS * HEAD_DIM
QK_WIDTH = ATTN_WIDTH + KV_WIDTH
KP_WIDTH = 2 * N_KV_HEADS * 128
IN_WIDTH = FNET_WIDTH + ATTN_WIDTH + 2 * KV_WIDTH
GRID_W = 64
AXIS_DIM = HEAD_DIM // 2
ROPE_THETA = 10000.0
EPS = 1e-6

LANES = 128
BF16_SUBLANES = 16
MXU_DIM = 256
VMEM_LIMIT_BYTES = 60000 * 1024

D_FF_PAD = ((D_FF + MXU_DIM - 1) // MXU_DIM) * MXU_DIM

TOKEN_TILE = 512
SUB_TILE = 256
ATTN_Q_TILE = 256

BF16 = jnp.bfloat16
F32 = jnp.float32


def _rms_norm(x, gain):
    return x * lax.rsqrt(jnp.mean(x * x, axis=-1, keepdims=True) + EPS) * gain


def _dot(a, b):
    return jnp.dot(a, b, preferred_element_type=F32)


def _swiglu_half_step(x, gain, wg_ref, wu_ref, wd_ref):
    xn = _rms_norm(x, gain).astype(BF16)
    g = _dot(xn, wg_ref[...])
    u = _dot(xn, wu_ref[...])
    a = (g * jax.nn.sigmoid(g) * u).astype(BF16)
    return x + 0.5 * _dot(a, wd_ref[...])


def _split_hi_lo(x):
    hi = x.astype(BF16)
    lo = (x - hi.astype(F32)).astype(BF16)
    return hi, lo


def _ffn1_inproj_kernel(x_ref, g1_ref, wg_ref, wu_ref, wd_ref, gm_ref, win_ref, gqk_ref,
                        cos_ref, sin_ref, seg_ref,
                        x1_ref, uf_ref, q_ref, kp_ref, vt_ref):
    seg = seg_ref[...]
    lane = lax.broadcasted_iota(jnp.int32, (1, LANES), 1)
    upper_half = (lane & (AXIS_DIM // 2)) != 0
    for r in range(TOKEN_TILE // SUB_TILE):
        rows = slice(r * SUB_TILE, (r + 1) * SUB_TILE)
        x1 = _swiglu_half_step(x_ref[rows, :], g1_ref[...], wg_ref, wu_ref, wd_ref)
        x1_ref[rows, :] = x1

        h = _rms_norm(x1, gm_ref[...]).astype(BF16)
        u = _dot(h, win_ref[...])
        uf_ref[rows, :] = u[:, :FNET_WIDTH].astype(BF16)
        vt_ref[:, rows] = u[:, FNET_WIDTH + QK_WIDTH:].T.astype(BF16)

        cos_t = cos_ref[rows, :]
        sin_t = sin_ref[rows, :]
        for c in range(QK_WIDTH // LANES):
            lo_col = FNET_WIDTH + c * LANES
            z = u[:, lo_col:lo_col + LANES]
            hi, lo = _split_hi_lo(z * z)
            mean_sq = (_dot(hi, seg) + _dot(lo, seg)) * (1.0 / HEAD_DIM)
            zn = z * lax.rsqrt(mean_sq + EPS) * gqk_ref[:, c * LANES:(c + 1) * LANES]
            partner = jnp.where(upper_half,
                                pltpu.roll(zn, AXIS_DIM // 2, axis=1),
                                pltpu.roll(zn, LANES - AXIS_DIM // 2, axis=1))
            out = zn * cos_t + partner * sin_t
            if c < ATTN_WIDTH // LANES:
                q_ref[rows, c * LANES:(c + 1) * LANES] = out.astype(BF16)
            else:
                swapped = pltpu.roll(out, HEAD_DIM, axis=1)
                first = lane < HEAD_DIM
                variants = (jnp.where(first, out, 0.0), jnp.where(first, 0.0, swapped),
                            jnp.where(first, swapped, 0.0), jnp.where(first, 0.0, out))
                for i, kv in enumerate(variants):
                    kp_ref[rows, i * LANES:(i + 1) * LANES] = kv.astype(BF16)


def _fourier_kernel(uf_ref, dft_ref, cc_ref, sc_ref, fw_ref, fb_ref, out_ref, y_ref):
    seq = uf_ref.shape[0]
    cc_hi, cc_lo = _split_hi_lo(cc_ref[...])
    sc_hi, sc_lo = _split_hi_lo(sc_ref[...])
    for g in range(FNET_GROUPS):
        w_hi, w_lo = _split_hi_lo(fw_ref[g])
        wc = (_dot(cc_hi, w_hi) + _dot(cc_hi, w_lo) + _dot(cc_lo, w_hi)).astype(BF16)
        ws = (_dot(sc_hi, w_hi) + _dot(sc_hi, w_lo) + _dot(sc_lo, w_hi)).astype(BF16)
        ug = uf_ref[:, g * FNET_GROUP_DIM:(g + 1) * FNET_GROUP_DIM]
        y_ref[:seq, g * FNET_GROUP_DIM:(g + 1) * FNET_GROUP_DIM] = _dot(ug, wc).astype(BF16)
        y_ref[seq:, g * FNET_GROUP_DIM:(g + 1) * FNET_GROUP_DIM] = _dot(ug, ws).astype(BF16)
    out_ref[...] = (_dot(dft_ref[...], y_ref[...]) + fb_ref[...]).astype(BF16)


def _attention_kernel(q_ref, kp_ref, vt_ref, o_ref):
    seq = kp_ref.shape[0]
    tq = ATTN_Q_TILE
    ones_rows = jnp.ones((BF16_SUBLANES, seq), BF16)

    def scores(qrows, g, half):
        qg = jnp.concatenate([q_ref[qrows, (2 * g + c) * LANES:(2 * g + c + 1) * LANES] for c in range(2)], axis=0)
        kpad = kp_ref[:, (2 * g + half) * LANES:(2 * g + half + 1) * LANES]
        return lax.dot_general(kpad, qg, (((1,), (1,)), ((), ())), preferred_element_type=F32)

    def q_tile(t, carry):
        qrows = pl.ds(pl.multiple_of(t * tq, tq), tq)
        units = [(g, half) for g in range(N_KV_HEADS) for half in range(2)]
        att = {}
        s_next = scores(qrows, *units[0])
        for i, (g, half) in enumerate(units):
            s = s_next
            if i + 1 < len(units):
                s_next = scores(qrows, *units[i + 1])
            v1t = jnp.concatenate([vt_ref[g * HEAD_DIM:(g + 1) * HEAD_DIM, :], ones_rows], axis=0)
            p = jnp.exp2(s - jnp.max(s, axis=0, keepdims=True)).astype(BF16)
            o = _dot(v1t, p)
            att[g, half] = o[:HEAD_DIM] * (1.0 / o[HEAD_DIM:HEAD_DIM + 1])
        for g in range(N_KV_HEADS):
            for c in range(2):
                pair = jnp.concatenate([att[g, half][:, c * tq:(c + 1) * tq] for half in range(2)], axis=0)
                o_ref[qrows, (2 * g + c) * LANES:(2 * g + c + 1) * LANES] = pair.T.astype(BF16)
        return carry

    lax.fori_loop(0, seq // tq, q_tile, 0)


def _outproj_ffn2_kernel(x1_ref, f_ref, a_ref, wof_ref, woa_ref, g2_ref, wg_ref, wu_ref, wd_ref,
                         gf_ref, y_ref):
    for r in range(TOKEN_TILE // SUB_TILE):
        rows = slice(r * SUB_TILE, (r + 1) * SUB_TILE)
        x2 = x1_ref[rows, :] + _dot(f_ref[rows, :], wof_ref[...]) + _dot(a_ref[rows, :], woa_ref[...])
        x3 = _swiglu_half_step(x2, g2_ref[...], wg_ref, wu_ref, wd_ref)
        y_ref[rows, :] = _rms_norm(x3, gf_ref[...])


def _resident(shape):
    return pl.BlockSpec(shape, lambda *_: (0,) * len(shape), pipeline_mode=pl.Buffered(1))


def _rope_tables(seq):
    rows = seq // GRID_W
    row_idx = jnp.repeat(jnp.arange(rows, dtype=F32), GRID_W)
    col_idx = jnp.tile(jnp.arange(GRID_W, dtype=F32), rows)
    inv_freq = ROPE_THETA ** (-jnp.arange(0, AXIS_DIM, 2, dtype=F32) / AXIS_DIM)
    ang_r = row_idx[:, None] * inv_freq[None, :]
    ang_c = col_idx[:, None] * inv_freq[None, :]
    cos_h = jnp.concatenate([jnp.cos(ang_r)] * 2 + [jnp.cos(ang_c)] * 2, axis=-1)
    sin_h = jnp.concatenate([-jnp.sin(ang_r), jnp.sin(ang_r), -jnp.sin(ang_c), jnp.sin(ang_c)], axis=-1)
    reps = LANES // HEAD_DIM
    return jnp.tile(cos_h, (1, reps)), jnp.tile(sin_h, (1, reps))


def _dft_cos_sin(n):
    idx = jnp.arange(n, dtype=jnp.int32)
    ang = ((idx[:, None] * idx[None, :]) % n).astype(F32) * (2.0 * math.pi / n)
    return jnp.cos(ang), jnp.sin(ang)


def _pad_ff_cols(w):
    return jnp.pad(w, ((0, 0), (0, D_FF_PAD - D_FF))).astype(BF16)


def _pad_ff_rows(w):
    return jnp.pad(w, ((0, D_FF_PAD - D_FF), (0, 0))).astype(BF16)


def _row(v):
    return v.reshape(1, -1).astype(F32)


def kernel(x, ffn1_norm, ffn1_w_gate, ffn1_w_up, ffn1_w_down, mix_norm, w_in, fnet_w, fnet_b, q_norm,
           k_norm, w_out, ffn2_norm, ffn2_w_gate, ffn2_w_up, ffn2_w_down, final_norm):
    batch, seq, d = x.shape
    assert d == D_MODEL and seq % TOKEN_TILE == 0 and seq % ATTN_Q_TILE == 0 and seq % GRID_W == 0
    n_tok = batch * seq
    tiles_per_seq = seq // TOKEN_TILE
    x2d = x.reshape(n_tok, d)

    params = pltpu.CompilerParams(dimension_semantics=("arbitrary",), vmem_limit_bytes=VMEM_LIMIT_BYTES)
    tok = lambda width: pl.BlockSpec((TOKEN_TILE, width), lambda i: (i, 0))

    cos_t, sin_t = _rope_tables(seq)
    scale = math.log2(math.e) / math.sqrt(HEAD_DIM)
    gain_qk = jnp.concatenate([jnp.tile(q_norm.astype(F32) * scale, N_Q_HEADS),
                               jnp.tile(k_norm.astype(F32), N_KV_HEADS)]).reshape(1, QK_WIDTH)
    head_of_lane = jnp.arange(LANES) // HEAD_DIM
    seg_ones = (head_of_lane[:, None] == head_of_lane[None, :]).astype(BF16)
    rope_spec = pl.BlockSpec((TOKEN_TILE, LANES), lambda i: (i % tiles_per_seq, 0))

    vt_spec = pl.BlockSpec((None, KV_WIDTH, TOKEN_TILE), lambda i: (i // tiles_per_seq, 0, i % tiles_per_seq))
    x1, uf, q, kp, vt = pl.pallas_call(
        _ffn1_inproj_kernel,
        grid=(n_tok // TOKEN_TILE,),
        in_specs=[tok(d), _resident((1, d)),
                  _resident((d, D_FF_PAD)), _resident((d, D_FF_PAD)), _resident((D_FF_PAD, d)),
                  _resident((1, d)), _resident((d, IN_WIDTH)), _resident((1, QK_WIDTH)),
                  rope_spec, rope_spec, _resident((LANES, LANES))],
        out_specs=[tok(d), tok(FNET_WIDTH), tok(ATTN_WIDTH), tok(KP_WIDTH), vt_spec],
        out_shape=[jax.ShapeDtypeStruct((n_tok, d), F32),
                   jax.ShapeDtypeStruct((n_tok, FNET_WIDTH), BF16),
                   jax.ShapeDtypeStruct((n_tok, ATTN_WIDTH), BF16),
                   jax.ShapeDtypeStruct((n_tok, KP_WIDTH), BF16),
                   jax.ShapeDtypeStruct((batch, KV_WIDTH, seq), BF16)],
        compiler_params=params,
        name="ffn1_inproj",
    )(x2d, _row(ffn1_norm), _pad_ff_cols(ffn1_w_gate), _pad_ff_cols(ffn1_w_up), _pad_ff_rows(ffn1_w_down),
      _row(mix_norm), w_in.astype(BF16), gain_qk, cos_t, sin_t, seg_ones)

    cos_s, sin_s = _dft_cos_sin(seq)
    dft = jnp.concatenate([cos_s, -sin_s], axis=1).astype(BF16)
    cos_c, sin_c = _dft_cos_sin(FNET_GROUP_DIM)
    ortho = 1.0 / math.sqrt(seq * FNET_GROUP_DIM)
    f_out = pl.pallas_call(
        _fourier_kernel,
        grid=(batch,),
        in_specs=[pl.BlockSpec((None, seq, FNET_WIDTH), lambda b: (b, 0, 0)),
                  _resident((seq, 2 * seq)),
                  _resident((FNET_GROUP_DIM, FNET_GROUP_DIM)), _resident((FNET_GROUP_DIM, FNET_GROUP_DIM)),
                  _resident((FNET_GROUPS, FNET_GROUP_DIM, FNET_GROUP_DIM)), _resident((1, FNET_WIDTH))],
        out_specs=pl.BlockSpec((None, seq, FNET_WIDTH), lambda b: (b, 0, 0)),
        out_shape=jax.ShapeDtypeStruct((batch, seq, FNET_WIDTH), BF16),
        scratch_shapes=[pltpu.VMEM((2 * seq, FNET_WIDTH), BF16)],
        compiler_params=params,
        name="fourier",
    )(uf.reshape(batch, seq, FNET_WIDTH), dft, cos_c * ortho, sin_c * ortho, fnet_w.astype(F32),
      fnet_b.reshape(1, FNET_WIDTH).astype(F32))

    a_out = pl.pallas_call(
        _attention_kernel,
        grid=(batch,),
        in_specs=[pl.BlockSpec((None, seq, ATTN_WIDTH), lambda b: (b, 0, 0)),
                  pl.BlockSpec((None, seq, KP_WIDTH), lambda b: (b, 0, 0)),
                  pl.BlockSpec((None, KV_WIDTH, seq), lambda b: (b, 0, 0))],
        out_specs=pl.BlockSpec((None, seq, ATTN_WIDTH), lambda b: (b, 0, 0)),
        out_shape=jax.ShapeDtypeStruct((batch, seq, ATTN_WIDTH), BF16),
        compiler_params=params,
        name="attention",
    )(q.reshape(batch, seq, ATTN_WIDTH), kp.reshape(batch, seq, KP_WIDTH), vt)

    w_out_bf = w_out.astype(BF16)
    y = pl.pallas_call(
        _outproj_ffn2_kernel,
        grid=(n_tok // TOKEN_TILE,),
        in_specs=[tok(d), tok(FNET_WIDTH), tok(ATTN_WIDTH),
                  _resident((FNET_WIDTH, d)), _resident((ATTN_WIDTH, d)), _resident((1, d)),
                  _resident((d, D_FF_PAD)), _resident((d, D_FF_PAD)), _resident((D_FF_PAD, d)),
                  _resident((1, d))],
        out_specs=tok(d),
        out_shape=jax.ShapeDtypeStruct((n_tok, d), F32),
        compiler_params=params,
        name="outproj_ffn2",
    )(x1, f_out.reshape(n_tok, FNET_WIDTH), a_out.reshape(n_tok, ATTN_WIDTH),
      w_out_bf[:FNET_WIDTH], w_out_bf[FNET_WIDTH:], _row(ffn2_norm),
      _pad_ff_cols(ffn2_w_gate), _pad_ff_cols(ffn2_w_up), _pad_ff_rows(ffn2_w_down), _row(final_norm))
    return y.reshape(batch, seq, d)
```

```python
import functools
import math

import jax
import jax.numpy as jnp
from jax import lax
from jax.experimental import pallas as pl
from jax.experimental.pallas import tpu as pltpu

D_MODEL = 1024
D_FF = 2752
FNET_WIDTH = 512
FNET_GROUPS = 4
FNET_GROUP_DIM = 128
HEAD_DIM = 64
N_Q_HEADS = 8
N_KV_HEADS = 2
GQA_GROUP = N_Q_HEADS // N_KV_HEADS
ATTN_WIDTH = N_Q_HEADS * HEAD_DIM
KV_WIDTH = N_KV_HEADS * HEAD_DIM
QK_WIDTH = ATTN_WIDTH + KV_WIDTH
KP_WIDTH = 2 * N_KV_HEADS * 128
IN_WIDTH = FNET_WIDTH + ATTN_WIDTH + 2 * KV_WIDTH
GRID_W = 64
AXIS_DIM = HEAD_DIM // 2
ROPE_THETA = 10000.0
EPS = 1e-6

LANES = 128
BF16_SUBLANES = 16
MXU_DIM = 256
VMEM_LIMIT_BYTES = 60000 * 1024

D_FF_PAD = ((D_FF + MXU_DIM - 1) // MXU_DIM) * MXU_DIM

TOKEN_TILE = 512
SUB_TILE = 256
ATTN_Q_TILE = 256

BF16 = jnp.bfloat16
F32 = jnp.float32


def _rms_norm(x, gain):
    return x * lax.rsqrt(jnp.mean(x * x, axis=-1, keepdims=True) + EPS) * gain


def _dot(a, b):
    return jnp.dot(a, b, preferred_element_type=F32)


def _swiglu_half_step(x, gain, wg_ref, wu_ref, wd_ref):
    xn = _rms_norm(x, gain).astype(BF16)
    g = _dot(xn, wg_ref[...])
    u = _dot(xn, wu_ref[...])
    a = (g * jax.nn.sigmoid(g) * u).astype(BF16)
    return x + 0.5 * _dot(a, wd_ref[...])


def _split_hi_lo(x):
    hi = x.astype(BF16)
    lo = (x - hi.astype(F32)).astype(BF16)
    return hi, lo


def _ffn1_inproj_kernel(x_ref, g1_ref, wg_ref, wu_ref, wd_ref, gm_ref, win_ref, gqk_ref,
                        cos_ref, sin_ref, seg_ref,
                        x1_ref, uf_ref, q_ref, kp_ref, vt_ref):
    seg = seg_ref[...]
    lane = lax.broadcasted_iota(jnp.int32, (1, LANES), 1)
    upper_half = (lane & (AXIS_DIM // 2)) != 0
    for r in range(TOKEN_TILE // SUB_TILE):
        rows = slice(r * SUB_TILE, (r + 1) * SUB_TILE)
        x1 = _swiglu_half_step(x_ref[rows, :], g1_ref[...], wg_ref, wu_ref, wd_ref)
        x1_ref[rows, :] = x1

        h = _rms_norm(x1, gm_ref[...]).astype(BF16)
        u = _dot(h, win_ref[...])
        uf_ref[rows, :] = u[:, :FNET_WIDTH].astype(BF16)
        vt = u[:, FNET_WIDTH + QK_WIDTH:].T.astype(BF16)
        for g in range(N_KV_HEADS):
            vt_ref[g, :, rows] = vt[g * HEAD_DIM:(g + 1) * HEAD_DIM]

        cos_t = cos_ref[rows, :]
        sin_t = sin_ref[rows, :]
        for c in range(QK_WIDTH // LANES):
            lo_col = FNET_WIDTH + c * LANES
            z = u[:, lo_col:lo_col + LANES]
            hi, lo = _split_hi_lo(z * z)
            mean_sq = (_dot(hi, seg) + _dot(lo, seg)) * (1.0 / HEAD_DIM)
            zn = z * lax.rsqrt(mean_sq + EPS) * gqk_ref[:, c * LANES:(c + 1) * LANES]
            partner = jnp.where(upper_half,
                                pltpu.roll(zn, AXIS_DIM // 2, axis=1),
                                pltpu.roll(zn, LANES - AXIS_DIM // 2, axis=1))
            out = zn * cos_t + partner * sin_t
            if c < ATTN_WIDTH // LANES:
                q_ref[c // 2, rows, (c % 2) * LANES:(c % 2 + 1) * LANES] = out.astype(BF16)
            else:
                swapped = pltpu.roll(out, HEAD_DIM, axis=1)
                first = lane < HEAD_DIM
                variants = (jnp.where(first, out, 0.0), jnp.where(first, 0.0, swapped),
                            jnp.where(first, swapped, 0.0), jnp.where(first, 0.0, out))
                for i, kv in enumerate(variants):
                    kp_ref[i, rows, :] = kv.astype(BF16)


def _fourier_kernel(uf_ref, dft_ref, cc_ref, sc_ref, fw_ref, fb_ref, out_ref, y_ref):
    seq = uf_ref.shape[0]
    cc_hi, cc_lo = _split_hi_lo(cc_ref[...])
    sc_hi, sc_lo = _split_hi_lo(sc_ref[...])
    for g in range(FNET_GROUPS):
        w_hi, w_lo = _split_hi_lo(fw_ref[g])
        wc = (_dot(cc_hi, w_hi) + _dot(cc_hi, w_lo) + _dot(cc_lo, w_hi)).astype(BF16)
        ws = (_dot(sc_hi, w_hi) + _dot(sc_hi, w_lo) + _dot(sc_lo, w_hi)).astype(BF16)
        ug = uf_ref[:, g * FNET_GROUP_DIM:(g + 1) * FNET_GROUP_DIM]
        y_ref[:seq, g * FNET_GROUP_DIM:(g + 1) * FNET_GROUP_DIM] = _dot(ug, wc).astype(BF16)
        y_ref[seq:, g * FNET_GROUP_DIM:(g + 1) * FNET_GROUP_DIM] = _dot(ug, ws).astype(BF16)
    out_ref[...] = (_dot(dft_ref[...], y_ref[...]) + fb_ref[...]).astype(BF16)


def _attention_kernel(q_ref, kp_ref, vt_ref, o_ref, s0_ref, s1_ref, m_ref, p0_ref, p1_ref, att_ref):
    seq = kp_ref.shape[1]
    tq = ATTN_Q_TILE
    n_pairs = (seq // tq) * N_KV_HEADS
    s_refs, p_refs = (s0_ref, s1_ref), (p0_ref, p1_ref)
    ones_rows = jnp.ones((BF16_SUBLANES, seq), BF16)

    def pair_index(k):
        t, g = k // N_KV_HEADS, k % N_KV_HEADS
        start = t * tq
        if not isinstance(start, int):
            start = pl.multiple_of(start, tq)
        return pl.ds(start, tq), g

    def qk(k, half):
        qrows, g = pair_index(k)
        qblk = q_ref[g, qrows, :]
        qg = jnp.concatenate([qblk[:, :LANES], qblk[:, LANES:]], axis=0)
        s = lax.dot_general(kp_ref[2 * g + half], qg, (((1,), (1,)), ((), ())), preferred_element_type=F32)
        s_refs[half][...] = s
        m_ref[half] = jnp.max(s, axis=0, keepdims=True)

    def ex(half):
        p_refs[half][...] = jnp.exp2(s_refs[half][...] - m_ref[half]).astype(BF16)

    def pv(k, half):
        _, g = pair_index(k)
        v1t = jnp.concatenate([vt_ref[g], ones_rows], axis=0)
        o = _dot(v1t, p_refs[half][...])
        att_ref[half] = o[:HEAD_DIM] * (1.0 / o[HEAD_DIM:HEAD_DIM + 1])

    def emit(k):
        qrows, g = pair_index(k)
        for c in range(2):
            pair = jnp.concatenate([att_ref[half, :, c * tq:(c + 1) * tq] for half in range(2)], axis=0)
            o_ref[g, qrows, c * LANES:(c + 1) * LANES] = pair.T.astype(BF16)

    qk(0, 0)
    qk(0, 1)
    ex(0)

    def step(k, carry):
        qk(k, 0)
        ex(1)
        pv(k - 1, 0)
        qk(k, 1)
        ex(0)
        pv(k - 1, 1)
        emit(k - 1)
        return carry

    lax.fori_loop(1, n_pairs, step, 0)
    ex(1)
    pv(n_pairs - 1, 0)
    pv(n_pairs - 1, 1)
    emit(n_pairs - 1)


def _outproj_ffn2_kernel(x1_ref, f_ref, a_ref, wof_ref, woa_ref, g2_ref, wg_ref, wu_ref, wd_ref,
                         gf_ref, y_ref):
    for r in range(TOKEN_TILE // SUB_TILE):
        rows = slice(r * SUB_TILE, (r + 1) * SUB_TILE)
        x2 = x1_ref[rows, :] + _dot(f_ref[rows, :], wof_ref[...])
        for g in range(N_KV_HEADS):
            x2 = x2 + _dot(a_ref[g, rows, :], woa_ref[g])
        x3 = _swiglu_half_step(x2, g2_ref[...], wg_ref, wu_ref, wd_ref)
        y_ref[rows, :] = _rms_norm(x3, gf_ref[...])


def _resident(shape):
    return pl.BlockSpec(shape, lambda *_: (0,) * len(shape), pipeline_mode=pl.Buffered(1))


def _rope_tables(seq):
    rows = seq // GRID_W
    row_idx = jnp.repeat(jnp.arange(rows, dtype=F32), GRID_W)
    col_idx = jnp.tile(jnp.arange(GRID_W, dtype=F32), rows)
    inv_freq = ROPE_THETA ** (-jnp.arange(0, AXIS_DIM, 2, dtype=F32) / AXIS_DIM)
    ang_r = row_idx[:, None] * inv_freq[None, :]
    ang_c = col_idx[:, None] * inv_freq[None, :]
    cos_h = jnp.concatenate([jnp.cos(ang_r)] * 2 + [jnp.cos(ang_c)] * 2, axis=-1)
    sin_h = jnp.concatenate([-jnp.sin(ang_r), jnp.sin(ang_r), -jnp.sin(ang_c), jnp.sin(ang_c)], axis=-1)
    reps = LANES // HEAD_DIM
    return jnp.tile(cos_h, (1, reps)), jnp.tile(sin_h, (1, reps))


def _dft_cos_sin(n):
    idx = jnp.arange(n, dtype=jnp.int32)
    ang = ((idx[:, None] * idx[None, :]) % n).astype(F32) * (2.0 * math.pi / n)
    return jnp.cos(ang), jnp.sin(ang)


def _pad_ff_cols(w):
    return jnp.pad(w, ((0, 0), (0, D_FF_PAD - D_FF))).astype(BF16)


def _pad_ff_rows(w):
    return jnp.pad(w, ((0, D_FF_PAD - D_FF), (0, 0))).astype(BF16)


def _row(v):
    return v.reshape(1, -1).astype(F32)


def kernel(x, ffn1_norm, ffn1_w_gate, ffn1_w_up, ffn1_w_down, mix_norm, w_in, fnet_w, fnet_b, q_norm,
           k_norm, w_out, ffn2_norm, ffn2_w_gate, ffn2_w_up, ffn2_w_down, final_norm):
    batch, seq, d = x.shape
    assert d == D_MODEL and seq % TOKEN_TILE == 0 and seq % ATTN_Q_TILE == 0 and seq % GRID_W == 0
    n_tok = batch * seq
    tiles_per_seq = seq // TOKEN_TILE
    x2d = x.reshape(n_tok, d)

    params = pltpu.CompilerParams(dimension_semantics=("arbitrary",), vmem_limit_bytes=VMEM_LIMIT_BYTES)
    tok = lambda width: pl.BlockSpec((TOKEN_TILE, width), lambda i: (i, 0))

    cos_t, sin_t = _rope_tables(seq)
    scale = math.log2(math.e) / math.sqrt(HEAD_DIM)
    gain_qk = jnp.concatenate([jnp.tile(q_norm.astype(F32) * scale, N_Q_HEADS),
                               jnp.tile(k_norm.astype(F32), N_KV_HEADS)]).reshape(1, QK_WIDTH)
    head_of_lane = jnp.arange(LANES) // HEAD_DIM
    seg_ones = (head_of_lane[:, None] == head_of_lane[None, :]).astype(BF16)
    rope_spec = pl.BlockSpec((TOKEN_TILE, LANES), lambda i: (i % tiles_per_seq, 0))

    grp = lambda n, width: pl.BlockSpec((n, TOKEN_TILE, width), lambda i: (0, i, 0))
    vt_spec = pl.BlockSpec((None, N_KV_HEADS, HEAD_DIM, TOKEN_TILE),
                           lambda i: (i // tiles_per_seq, 0, 0, i % tiles_per_seq))
    x1, uf, q, kp, vt = pl.pallas_call(
        _ffn1_inproj_kernel,
        grid=(n_tok // TOKEN_TILE,),
        in_specs=[tok(d), _resident((1, d)),
                  _resident((d, D_FF_PAD)), _resident((d, D_FF_PAD)), _resident((D_FF_PAD, d)),
                  _resident((1, d)), _resident((d, IN_WIDTH)), _resident((1, QK_WIDTH)),
                  rope_spec, rope_spec, _resident((LANES, LANES))],
        out_specs=[tok(d), tok(FNET_WIDTH), grp(N_KV_HEADS, 2 * LANES), grp(2 * N_KV_HEADS, LANES), vt_spec],
        out_shape=[jax.ShapeDtypeStruct((n_tok, d), F32),
                   jax.ShapeDtypeStruct((n_tok, FNET_WIDTH), BF16),
                   jax.ShapeDtypeStruct((N_KV_HEADS, n_tok, 2 * LANES), BF16),
                   jax.ShapeDtypeStruct((2 * N_KV_HEADS, n_tok, LANES), BF16),
                   jax.ShapeDtypeStruct((batch, N_KV_HEADS, HEAD_DIM, seq), BF16)],
        compiler_params=params,
        name="ffn1_inproj",
    )(x2d, _row(ffn1_norm), _pad_ff_cols(ffn1_w_gate), _pad_ff_cols(ffn1_w_up), _pad_ff_rows(ffn1_w_down),
      _row(mix_norm), w_in.astype(BF16), gain_qk, cos_t, sin_t, seg_ones)

    cos_s, sin_s = _dft_cos_sin(seq)
    dft = jnp.concatenate([cos_s, -sin_s], axis=1).astype(BF16)
    cos_c, sin_c = _dft_cos_sin(FNET_GROUP_DIM)
    ortho = 1.0 / math.sqrt(seq * FNET_GROUP_DIM)
    f_out = pl.pallas_call(
        _fourier_kernel,
        grid=(batch,),
        in_specs=[pl.BlockSpec((None, seq, FNET_WIDTH), lambda b: (b, 0, 0)),
                  _resident((seq, 2 * seq)),
                  _resident((FNET_GROUP_DIM, FNET_GROUP_DIM)), _resident((FNET_GROUP_DIM, FNET_GROUP_DIM)),
                  _resident((FNET_GROUPS, FNET_GROUP_DIM, FNET_GROUP_DIM)), _resident((1, FNET_WIDTH))],
        out_specs=pl.BlockSpec((None, seq, FNET_WIDTH), lambda b: (b, 0, 0)),
        out_shape=jax.ShapeDtypeStruct((batch, seq, FNET_WIDTH), BF16),
        scratch_shapes=[pltpu.VMEM((2 * seq, FNET_WIDTH), BF16)],
        compiler_params=params,
        name="fourier",
    )(uf.reshape(batch, seq, FNET_WIDTH), dft, cos_c * ortho, sin_c * ortho, fnet_w.astype(F32),
      fnet_b.reshape(1, FNET_WIDTH).astype(F32))

    a_out = pl.pallas_call(
        _attention_kernel,
        grid=(batch,),
        in_specs=[pl.BlockSpec((N_KV_HEADS, seq, 2 * LANES), lambda b: (0, b, 0)),
                  pl.BlockSpec((2 * N_KV_HEADS, seq, LANES), lambda b: (0, b, 0)),
                  pl.BlockSpec((None, N_KV_HEADS, HEAD_DIM, seq), lambda b: (b, 0, 0, 0))],
        out_specs=pl.BlockSpec((N_KV_HEADS, seq, 2 * LANES), lambda b: (0, b, 0)),
        out_shape=jax.ShapeDtypeStruct((N_KV_HEADS, n_tok, 2 * LANES), BF16),
        scratch_shapes=[pltpu.VMEM((seq, 2 * ATTN_Q_TILE), F32), pltpu.VMEM((seq, 2 * ATTN_Q_TILE), F32),
                        pltpu.VMEM((2, 1, 2 * ATTN_Q_TILE), F32),
                        pltpu.VMEM((seq, 2 * ATTN_Q_TILE), BF16), pltpu.VMEM((seq, 2 * ATTN_Q_TILE), BF16),
                        pltpu.VMEM((2, HEAD_DIM, 2 * ATTN_Q_TILE), F32)],
        compiler_params=params,
        name="attention",
    )(q, kp, vt)

    w_out_bf = w_out.astype(BF16)
    y = pl.pallas_call(
        _outproj_ffn2_kernel,
        grid=(n_tok // TOKEN_TILE,),
        in_specs=[tok(d), tok(FNET_WIDTH), grp(N_KV_HEADS, 2 * LANES),
                  _resident((FNET_WIDTH, d)), _resident((N_KV_HEADS, 2 * LANES, d)), _resident((1, d)),
                  _resident((d, D_FF_PAD)), _resident((d, D_FF_PAD)), _resident((D_FF_PAD, d)),
                  _resident((1, d))],
        out_specs=tok(d),
        out_shape=jax.ShapeDtypeStruct((n_tok, d), F32),
        compiler_params=params,
        name="outproj_ffn2",
    )(x1, f_out.reshape(n_tok, FNET_WIDTH), a_out,
      w_out_bf[:FNET_WIDTH], w_out_bf[FNET_WIDTH:].reshape(N_KV_HEADS, 2 * LANES, d), _row(ffn2_norm),
      _pad_ff_cols(ffn2_w_gate), _pad_ff_cols(ffn2_w_up), _pad_ff_rows(ffn2_w_down), _row(final_norm))
    return y.reshape(batch, seq, d)
```

```python
import functools
import math

import jax
import jax.numpy as jnp
from jax import lax
from jax.experimental import pallas as pl
from jax.experimental.pallas import tpu as pltpu

D_MODEL = 1024
D_FF = 2752
FNET_WIDTH = 512
FNET_GROUPS = 4
FNET_GROUP_DIM = 128
HEAD_DIM = 64
N_Q_HEADS = 8
N_KV_HEADS = 2
GQA_GROUP = N_Q_HEADS // N_KV_HEADS
ATTN_WIDTH = N_Q_HEADS * HEAD_DIM
KV_WIDTH = N_KV_HEADS * HEAD_DIM
QK_WIDTH = ATTN_WIDTH + KV_WIDTH
KP_WIDTH = 2 * N_KV_HEADS * 128
IN_WIDTH = FNET_WIDTH + ATTN_WIDTH + 2 * KV_WIDTH
GRID_W = 64
AXIS_DIM = HEAD_DIM // 2
ROPE_THETA = 10000.0
EPS = 1e-6

LANES = 128
BF16_SUBLANES = 16
MXU_DIM = 256
VMEM_LIMIT_BYTES = 60000 * 1024

D_FF_PAD = ((D_FF + MXU_DIM - 1) // MXU_DIM) * MXU_DIM

TOKEN_TILE = 512
SUB_TILE = 256
ATTN_Q_TILE = 256

BF16 = jnp.bfloat16
F32 = jnp.float32


def _rms_norm(x, gain):
    return x * lax.rsqrt(jnp.mean(x * x, axis=-1, keepdims=True) + EPS) * gain


def _dot(a, b):
    return jnp.dot(a, b, preferred_element_type=F32)


def _swiglu_half_step(x, gain, wg_ref, wu_ref, wd_ref):
    xn = _rms_norm(x, gain).astype(BF16)
    g = _dot(xn, wg_ref[...])
    u = _dot(xn, wu_ref[...])
    a = (g * jax.nn.sigmoid(g) * u).astype(BF16)
    return x + 0.5 * _dot(a, wd_ref[...])


def _split_hi_lo(x):
    hi = x.astype(BF16)
    lo = (x - hi.astype(F32)).astype(BF16)
    return hi, lo


def _ffn1_inproj_kernel(x_ref, g1_ref, wg_ref, wu_ref, wd_ref, gm_ref, win_ref, gqk_ref,
                        cos_ref, sin_ref, seg_ref,
                        x1_ref, uf_ref, q_ref, kp_ref, vt_ref):
    seg = seg_ref[...]
    lane = lax.broadcasted_iota(jnp.int32, (1, LANES), 1)
    upper_half = (lane & (AXIS_DIM // 2)) != 0
    for r in range(TOKEN_TILE // SUB_TILE):
        rows = slice(r * SUB_TILE, (r + 1) * SUB_TILE)
        x1 = _swiglu_half_step(x_ref[rows, :], g1_ref[...], wg_ref, wu_ref, wd_ref)
        x1_ref[rows, :] = x1

        h = _rms_norm(x1, gm_ref[...]).astype(BF16)
        u = _dot(h, win_ref[...])
        uf_ref[rows, :] = u[:, :FNET_WIDTH].astype(BF16)
        vt = u[:, FNET_WIDTH + QK_WIDTH:].T.astype(BF16)
        for g in range(N_KV_HEADS):
            vt_ref[g, :, rows] = vt[g * HEAD_DIM:(g + 1) * HEAD_DIM]

        cos_t = cos_ref[rows, :]
        sin_t = sin_ref[rows, :]
        for c in range(QK_WIDTH // LANES):
            lo_col = FNET_WIDTH + c * LANES
            z = u[:, lo_col:lo_col + LANES]
            hi, lo = _split_hi_lo(z * z)
            mean_sq = (_dot(hi, seg) + _dot(lo, seg)) * (1.0 / HEAD_DIM)
            zn = z * lax.rsqrt(mean_sq + EPS) * gqk_ref[:, c * LANES:(c + 1) * LANES]
            partner = jnp.where(upper_half,
                                pltpu.roll(zn, AXIS_DIM // 2, axis=1),
                                pltpu.roll(zn, LANES - AXIS_DIM // 2, axis=1))
            out = zn * cos_t + partner * sin_t
            if c < ATTN_WIDTH // LANES:
                q_ref[c // 2, rows, (c % 2) * LANES:(c % 2 + 1) * LANES] = out.astype(BF16)
            else:
                swapped = pltpu.roll(out, HEAD_DIM, axis=1)
                first = lane < HEAD_DIM
                variants = (jnp.where(first, out, 0.0), jnp.where(first, 0.0, swapped),
                            jnp.where(first, swapped, 0.0), jnp.where(first, 0.0, out))
                for i, kv in enumerate(variants):
                    kp_ref[i, rows, :] = kv.astype(BF16)


def _fourier_kernel(uf_ref, ch_ref, sh_ref, rev_ref, cc_ref, sc_ref, fw_ref, fb_ref, out_ref, e1_ref, o2_ref):
    n = uf_ref.shape[0]
    h = n // 2
    lo = uf_ref[:h, :]
    hi = uf_ref[h:, :]
    rev = rev_ref[...]
    mirrored = _dot(rev, hi)
    lo32 = lo.astype(F32)
    even = (lo32 + mirrored).astype(BF16)
    odd = (lo32 - mirrored).astype(BF16)

    cc_hi, cc_lo = _split_hi_lo(cc_ref[...])
    sc_hi, sc_lo = _split_hi_lo(sc_ref[...])
    mid_rows = []
    for g in range(FNET_GROUPS):
        cols = slice(g * FNET_GROUP_DIM, (g + 1) * FNET_GROUP_DIM)
        w_hi, w_lo = _split_hi_lo(fw_ref[g])
        wc = (_dot(cc_hi, w_hi) + _dot(cc_hi, w_lo) + _dot(cc_lo, w_hi)).astype(BF16)
        ws = (_dot(sc_hi, w_hi) + _dot(sc_hi, w_lo) + _dot(sc_lo, w_hi)).astype(BF16)
        e1_ref[:, cols] = _dot(even[:, cols], wc).astype(BF16)
        o2_ref[:, cols] = _dot(odd[:, cols], ws).astype(BF16)
        mid_rows.append(_dot(hi[:BF16_SUBLANES, cols], wc)[0:1])
    mid = jnp.concatenate(mid_rows, axis=-1)

    a = _dot(ch_ref[...], e1_ref[...])
    b = _dot(sh_ref[...], o2_ref[...])
    j = lax.broadcasted_iota(jnp.int32, (a.shape[0], 1), 0)
    a = a + jnp.where((j & 1) == 0, 1.0, -1.0) * mid + fb_ref[...]
    out_ref[:h, :] = (a[:h] - b[:h]).astype(BF16)
    z = (a + b).astype(BF16)
    out_ref[h:, :] = _dot(rev, z[:h]).astype(BF16)
    out_ref[h:h + 1, :] = z[h:h + 1]


def _attention_kernel(q_ref, kp_ref, vt_ref, o_ref, s0_ref, s1_ref, m_ref, p0_ref, p1_ref, att_ref):
    seq = kp_ref.shape[1]
    tq = ATTN_Q_TILE
    n_pairs = (seq // tq) * N_KV_HEADS
    s_refs, p_refs = (s0_ref, s1_ref), (p0_ref, p1_ref)
    ones_rows = jnp.ones((BF16_SUBLANES, seq), BF16)

    def pair_index(k):
        t, g = k // N_KV_HEADS, k % N_KV_HEADS
        start = t * tq
        if not isinstance(start, int):
            start = pl.multiple_of(start, tq)
        return pl.ds(start, tq), g

    def qk(k, half):
        qrows, g = pair_index(k)
        qblk = q_ref[g, qrows, :]
        qg = jnp.concatenate([qblk[:, :LANES], qblk[:, LANES:]], axis=0)
        s = lax.dot_general(kp_ref[2 * g + half], qg, (((1,), (1,)), ((), ())), preferred_element_type=F32)
        s_refs[half][...] = s
        m_ref[half] = jnp.max(s, axis=0, keepdims=True)

    def ex(half):
        p_refs[half][...] = jnp.exp2(s_refs[half][...] - m_ref[half]).astype(BF16)

    def pv(k, half):
        _, g = pair_index(k)
        v1t = jnp.concatenate([vt_ref[g], ones_rows], axis=0)
        o = _dot(v1t, p_refs[half][...])
        att_ref[half] = o[:HEAD_DIM] * (1.0 / o[HEAD_DIM:HEAD_DIM + 1])

    def emit(k):
        qrows, g = pair_index(k)
        for c in range(2):
            pair = jnp.concatenate([att_ref[half, :, c * tq:(c + 1) * tq] for half in range(2)], axis=0)
            o_ref[g, qrows, c * LANES:(c + 1) * LANES] = pair.T.astype(BF16)

    qk(0, 0)
    qk(0, 1)
    ex(0)

    def step(k, carry):
        qk(k, 0)
        ex(1)
        pv(k - 1, 0)
        qk(k, 1)
        ex(0)
        pv(k - 1, 1)
        emit(k - 1)
        return carry

    lax.fori_loop(1, n_pairs, step, 0)
    ex(1)
    pv(n_pairs - 1, 0)
    pv(n_pairs - 1, 1)
    emit(n_pairs - 1)


def _outproj_ffn2_kernel(x1_ref, f_ref, a_ref, wof_ref, woa_ref, g2_ref, wg_ref, wu_ref, wd_ref,
                         gf_ref, y_ref):
    for r in range(TOKEN_TILE // SUB_TILE):
        rows = slice(r * SUB_TILE, (r + 1) * SUB_TILE)
        x2 = x1_ref[rows, :] + _dot(f_ref[rows, :], wof_ref[...])
        for g in range(N_KV_HEADS):
            x2 = x2 + _dot(a_ref[g, rows, :], woa_ref[g])
        x3 = _swiglu_half_step(x2, g2_ref[...], wg_ref, wu_ref, wd_ref)
        y_ref[rows, :] = _rms_norm(x3, gf_ref[...])


def _resident(shape):
    return pl.BlockSpec(shape, lambda *_: (0,) * len(shape), pipeline_mode=pl.Buffered(1))


def _rope_tables(seq):
    rows = seq // GRID_W
    row_idx = jnp.repeat(jnp.arange(rows, dtype=F32), GRID_W)
    col_idx = jnp.tile(jnp.arange(GRID_W, dtype=F32), rows)
    inv_freq = ROPE_THETA ** (-jnp.arange(0, AXIS_DIM, 2, dtype=F32) / AXIS_DIM)
    ang_r = row_idx[:, None] * inv_freq[None, :]
    ang_c = col_idx[:, None] * inv_freq[None, :]
    cos_h = jnp.concatenate([jnp.cos(ang_r)] * 2 + [jnp.cos(ang_c)] * 2, axis=-1)
    sin_h = jnp.concatenate([-jnp.sin(ang_r), jnp.sin(ang_r), -jnp.sin(ang_c), jnp.sin(ang_c)], axis=-1)
    reps = LANES // HEAD_DIM
    return jnp.tile(cos_h, (1, reps)), jnp.tile(sin_h, (1, reps))


def _dft_cos_sin(n, rows, cols):
    j = jnp.arange(rows, dtype=jnp.int32)
    k = jnp.arange(cols, dtype=jnp.int32)
    ang = ((j[:, None] * k[None, :]) % n).astype(F32) * (2.0 * math.pi / n)
    return jnp.cos(ang), jnp.sin(ang)


def _pad_ff_cols(w):
    return jnp.pad(w, ((0, 0), (0, D_FF_PAD - D_FF))).astype(BF16)


def _pad_ff_rows(w):
    return jnp.pad(w, ((0, D_FF_PAD - D_FF), (0, 0))).astype(BF16)


def _row(v):
    return v.reshape(1, -1).astype(F32)


def kernel(x, ffn1_norm, ffn1_w_gate, ffn1_w_up, ffn1_w_down, mix_norm, w_in, fnet_w, fnet_b, q_norm,
           k_norm, w_out, ffn2_norm, ffn2_w_gate, ffn2_w_up, ffn2_w_down, final_norm):
    batch, seq, d = x.shape
    assert d == D_MODEL and seq % TOKEN_TILE == 0 and seq % ATTN_Q_TILE == 0 and seq % GRID_W == 0
    n_tok = batch * seq
    tiles_per_seq = seq // TOKEN_TILE
    x2d = x.reshape(n_tok, d)

    params = pltpu.CompilerParams(dimension_semantics=("arbitrary",), vmem_limit_bytes=VMEM_LIMIT_BYTES)
    tok = lambda width: pl.BlockSpec((TOKEN_TILE, width), lambda i: (i, 0))

    cos_t, sin_t = _rope_tables(seq)
    scale = math.log2(math.e) / math.sqrt(HEAD_DIM)
    gain_qk = jnp.concatenate([jnp.tile(q_norm.astype(F32) * scale, N_Q_HEADS),
                               jnp.tile(k_norm.astype(F32), N_KV_HEADS)]).reshape(1, QK_WIDTH)
    head_of_lane = jnp.arange(LANES) // HEAD_DIM
    seg_ones = (head_of_lane[:, None] == head_of_lane[None, :]).astype(BF16)
    rope_spec = pl.BlockSpec((TOKEN_TILE, LANES), lambda i: (i % tiles_per_seq, 0))

    grp = lambda n, width: pl.BlockSpec((n, TOKEN_TILE, width), lambda i: (0, i, 0))
    vt_spec = pl.BlockSpec((None, N_KV_HEADS, HEAD_DIM, TOKEN_TILE),
                           lambda i: (i // tiles_per_seq, 0, 0, i % tiles_per_seq))
    x1, uf, q, kp, vt = pl.pallas_call(
        _ffn1_inproj_kernel,
        grid=(n_tok // TOKEN_TILE,),
        in_specs=[tok(d), _resident((1, d)),
                  _resident((d, D_FF_PAD)), _resident((d, D_FF_PAD)), _resident((D_FF_PAD, d)),
                  _resident((1, d)), _resident((d, IN_WIDTH)), _resident((1, QK_WIDTH)),
                  rope_spec, rope_spec, _resident((LANES, LANES))],
        out_specs=[tok(d), tok(FNET_WIDTH), grp(N_KV_HEADS, 2 * LANES), grp(2 * N_KV_HEADS, LANES), vt_spec],
        out_shape=[jax.ShapeDtypeStruct((n_tok, d), F32),
                   jax.ShapeDtypeStruct((n_tok, FNET_WIDTH), BF16),
                   jax.ShapeDtypeStruct((N_KV_HEADS, n_tok, 2 * LANES), BF16),
                   jax.ShapeDtypeStruct((2 * N_KV_HEADS, n_tok, LANES), BF16),
                   jax.ShapeDtypeStruct((batch, N_KV_HEADS, HEAD_DIM, seq), BF16)],
        compiler_params=params,
        name="ffn1_inproj",
    )(x2d, _row(ffn1_norm), _pad_ff_cols(ffn1_w_gate), _pad_ff_cols(ffn1_w_up), _pad_ff_rows(ffn1_w_down),
      _row(mix_norm), w_in.astype(BF16), gain_qk, cos_t, sin_t, seg_ones)

    half = seq // 2
    dft_rows = half + BF16_SUBLANES
    cos_s, sin_s = _dft_cos_sin(seq, dft_rows, half)
    idx = jnp.arange(half, dtype=jnp.int32)
    rev = ((idx[:, None] + idx[None, :]) == half).astype(BF16)
    cos_c, sin_c = _dft_cos_sin(FNET_GROUP_DIM, FNET_GROUP_DIM, FNET_GROUP_DIM)
    ortho = 1.0 / math.sqrt(seq * FNET_GROUP_DIM)
    f_out = pl.pallas_call(
        _fourier_kernel,
        grid=(batch,),
        in_specs=[pl.BlockSpec((None, seq, FNET_WIDTH), lambda b: (b, 0, 0)),
                  _resident((dft_rows, half)), _resident((dft_rows, half)), _resident((half, half)),
                  _resident((FNET_GROUP_DIM, FNET_GROUP_DIM)), _resident((FNET_GROUP_DIM, FNET_GROUP_DIM)),
                  _resident((FNET_GROUPS, FNET_GROUP_DIM, FNET_GROUP_DIM)), _resident((1, FNET_WIDTH))],
        out_specs=pl.BlockSpec((None, seq, FNET_WIDTH), lambda b: (b, 0, 0)),
        out_shape=jax.ShapeDtypeStruct((batch, seq, FNET_WIDTH), BF16),
        scratch_shapes=[pltpu.VMEM((half, FNET_WIDTH), BF16), pltpu.VMEM((half, FNET_WIDTH), BF16)],
        compiler_params=params,
        name="fourier",
    )(uf.reshape(batch, seq, FNET_WIDTH), cos_s.astype(BF16), sin_s.astype(BF16), rev,
      cos_c * ortho, sin_c * ortho, fnet_w.astype(F32), fnet_b.reshape(1, FNET_WIDTH).astype(F32))

    a_out = pl.pallas_call(
        _attention_kernel,
        grid=(batch,),
        in_specs=[pl.BlockSpec((N_KV_HEADS, seq, 2 * LANES), lambda b: (0, b, 0)),
                  pl.BlockSpec((2 * N_KV_HEADS, seq, LANES), lambda b: (0, b, 0)),
                  pl.BlockSpec((None, N_KV_HEADS, HEAD_DIM, seq), lambda b: (b, 0, 0, 0))],
        out_specs=pl.BlockSpec((N_KV_HEADS, seq, 2 * LANES), lambda b: (0, b, 0)),
        out_shape=jax.ShapeDtypeStruct((N_KV_HEADS, n_tok, 2 * LANES), BF16),
        scratch_shapes=[pltpu.VMEM((seq, 2 * ATTN_Q_TILE), F32), pltpu.VMEM((seq, 2 * ATTN_Q_TILE), F32),
                        pltpu.VMEM((2, 1, 2 * ATTN_Q_TILE), F32),
                        pltpu.VMEM((seq, 2 * ATTN_Q_TILE), BF16), pltpu.VMEM((seq, 2 * ATTN_Q_TILE), BF16),
                        pltpu.VMEM((2, HEAD_DIM, 2 * ATTN_Q_TILE), F32)],
        compiler_params=params,
        name="attention",
    )(q, kp, vt)

    w_out_bf = w_out.astype(BF16)
    y = pl.pallas_call(
        _outproj_ffn2_kernel,
        grid=(n_tok // TOKEN_TILE,),
        in_specs=[tok(d), tok(FNET_WIDTH), grp(N_KV_HEADS, 2 * LANES),
                  _resident((FNET_WIDTH, d)), _resident((N_KV_HEADS, 2 * LANES, d)), _resident((1, d)),
                  _resident((d, D_FF_PAD)), _resident((d, D_FF_PAD)), _resident((D_FF_PAD, d)),
                  _resident((1, d))],
        out_specs=tok(d),
        out_shape=jax.ShapeDtypeStruct((n_tok, d), F32),
        compiler_params=params,
        name="outproj_ffn2",
    )(x1, f_out.reshape(n_tok, FNET_WIDTH), a_out,
      w_out_bf[:FNET_WIDTH], w_out_bf[FNET_WIDTH:].reshape(N_KV_HEADS, 2 * LANES, d), _row(ffn2_norm),
      _pad_ff_cols(ffn2_w_gate), _pad_ff_cols(ffn2_w_up), _pad_ff_rows(ffn2_w_down), _row(final_norm))
    return y.reshape(batch, seq, d)
```

```python
import functools
import math

import jax
import jax.numpy as jnp
from jax import lax
from jax.experimental import pallas as pl
from jax.experimental.pallas import tpu as pltpu

D_MODEL = 1024
D_FF = 2752
FNET_WIDTH = 512
FNET_GROUPS = 4
FNET_GROUP_DIM = 128
HEAD_DIM = 64
N_Q_HEADS = 8
N_KV_HEADS = 2
GQA_GROUP = N_Q_HEADS // N_KV_HEADS
ATTN_WIDTH = N_Q_HEADS * HEAD_DIM
KV_WIDTH = N_KV_HEADS * HEAD_DIM
QK_WIDTH = ATTN_WIDTH + KV_WIDTH
KP_WIDTH = 2 * N_KV_HEADS * 128
IN_WIDTH = FNET_WIDTH + ATTN_WIDTH + 2 * KV_WIDTH
GRID_W = 64
AXIS_DIM = HEAD_DIM // 2
ROPE_THETA = 10000.0
EPS = 1e-6

LANES = 128
BF16_SUBLANES = 16
MXU_DIM = 256
VMEM_LIMIT_BYTES = 60000 * 1024

D_FF_PAD = ((D_FF + MXU_DIM - 1) // MXU_DIM) * MXU_DIM

TOKEN_TILE = 512
SUB_TILE = 256
ATTN_Q_TILE = 256

BF16 = jnp.bfloat16
F32 = jnp.float32


def _rms_norm(x, gain):
    return x * lax.rsqrt(jnp.mean(x * x, axis=-1, keepdims=True) + EPS) * gain


def _dot(a, b):
    return jnp.dot(a, b, preferred_element_type=F32)


def _swiglu_half_step(x, gain, wg_ref, wu_ref, wd_ref):
    xn = _rms_norm(x, gain).astype(BF16)
    g = _dot(xn, wg_ref[...])
    u = _dot(xn, wu_ref[...])
    a = (g * jax.nn.sigmoid(g) * u).astype(BF16)
    return x + 0.5 * _dot(a, wd_ref[...])


def _split_hi_lo(x):
    hi = x.astype(BF16)
    lo = (x - hi.astype(F32)).astype(BF16)
    return hi, lo


def _ffn1_inproj_kernel(x_ref, g1_ref, wg_ref, wu_ref, wd_ref, gm_ref, win_ref, gqk_ref,
                        cos_ref, sin_ref, seg_ref,
                        x1_ref, uf_ref, q_ref, kp_ref, vt_ref):
    seg = seg_ref[...]
    lane = lax.broadcasted_iota(jnp.int32, (1, LANES), 1)
    upper_half = (lane & (AXIS_DIM // 2)) != 0
    for r in range(TOKEN_TILE // SUB_TILE):
        rows = slice(r * SUB_TILE, (r + 1) * SUB_TILE)
        x1 = _swiglu_half_step(x_ref[rows, :], g1_ref[...], wg_ref, wu_ref, wd_ref)
        x1_ref[rows, :] = x1

        h = _rms_norm(x1, gm_ref[...]).astype(BF16)
        u = _dot(h, win_ref[...])
        uf_ref[rows, :] = u[:, :FNET_WIDTH].astype(BF16)
        vt = u[:, FNET_WIDTH + QK_WIDTH:].T.astype(BF16)
        for g in range(N_KV_HEADS):
            vt_ref[g, :, rows] = vt[g * HEAD_DIM:(g + 1) * HEAD_DIM]

        cos_t = cos_ref[rows, :]
        sin_t = sin_ref[rows, :]
        for c in range(QK_WIDTH // LANES):
            lo_col = FNET_WIDTH + c * LANES
            z = u[:, lo_col:lo_col + LANES]
            hi, lo = _split_hi_lo(z * z)
            mean_sq = (_dot(hi, seg) + _dot(lo, seg)) * (1.0 / HEAD_DIM)
            zn = z * lax.rsqrt(mean_sq + EPS) * gqk_ref[:, c * LANES:(c + 1) * LANES]
            partner = jnp.where(upper_half,
                                pltpu.roll(zn, AXIS_DIM // 2, axis=1),
                                pltpu.roll(zn, LANES - AXIS_DIM // 2, axis=1))
            out = zn * cos_t + partner * sin_t
            if c < ATTN_WIDTH // LANES:
                q_ref[c // 2, rows, (c % 2) * LANES:(c % 2 + 1) * LANES] = out.astype(BF16)
            else:
                swapped = pltpu.roll(out, HEAD_DIM, axis=1)
                first = lane < HEAD_DIM
                variants = (jnp.where(first, out, 0.0), jnp.where(first, 0.0, swapped),
                            jnp.where(first, swapped, 0.0), jnp.where(first, 0.0, out))
                for i, kv in enumerate(variants):
                    kp_ref[i, rows, :] = kv.astype(BF16)


def _fourier_kernel(uf_ref, ch_ref, sh_ref, rev_ref, cc_ref, sc_ref, fw_ref, fb_ref, out_ref, e1_ref, o2_ref):
    n = uf_ref.shape[0]
    h = n // 2
    lo = uf_ref[:h, :]
    hi = uf_ref[h:, :]
    rev = rev_ref[...]
    mirrored = _dot(rev, hi)
    lo32 = lo.astype(F32)
    even = (lo32 + mirrored).astype(BF16)
    odd = (lo32 - mirrored).astype(BF16)

    cc_hi, cc_lo = _split_hi_lo(cc_ref[...])
    sc_hi, sc_lo = _split_hi_lo(sc_ref[...])
    mid_rows = []
    for g in range(FNET_GROUPS):
        cols = slice(g * FNET_GROUP_DIM, (g + 1) * FNET_GROUP_DIM)
        w_hi, w_lo = _split_hi_lo(fw_ref[g])
        wc = (_dot(cc_hi, w_hi) + _dot(cc_hi, w_lo) + _dot(cc_lo, w_hi)).astype(BF16)
        ws = (_dot(sc_hi, w_hi) + _dot(sc_hi, w_lo) + _dot(sc_lo, w_hi)).astype(BF16)
        e1_ref[:, cols] = _dot(even[:, cols], wc).astype(BF16)
        o2_ref[:, cols] = _dot(odd[:, cols], ws).astype(BF16)
        mid_rows.append(_dot(hi[:BF16_SUBLANES, cols], wc)[0:1])
    mid = jnp.concatenate(mid_rows, axis=-1)

    a = _dot(ch_ref[...], e1_ref[...])
    b = _dot(sh_ref[...], o2_ref[...])
    j = lax.broadcasted_iota(jnp.int32, (a.shape[0], 1), 0)
    a = a + jnp.where((j & 1) == 0, 1.0, -1.0) * mid + fb_ref[...]
    out_ref[:h, :] = (a[:h] - b[:h]).astype(BF16)
    z = (a + b).astype(BF16)
    out_ref[h:, :] = _dot(rev, z[:h]).astype(BF16)
    out_ref[h:h + 1, :] = z[h:h + 1]


def _attention_kernel(q_ref, kp_ref, vt_ref, o_ref, s_ref, p_ref, m_ref, att_ref):
    seq = kp_ref.shape[1]
    tq = ATTN_Q_TILE
    units_per_tile = 2 * N_KV_HEADS
    n_units = (seq // tq) * units_per_tile
    ones_rows = jnp.ones((BF16_SUBLANES, seq), BF16)

    def unit_index(u):
        t, r = u // units_per_tile, u % units_per_tile
        start = t * tq
        if not isinstance(start, int):
            start = pl.multiple_of(start, tq)
        return pl.ds(start, tq), r // 2, r % 2

    def qk(u):
        q_rows, g, half = unit_index(u)
        qblk = q_ref[g, q_rows, :]
        qg = jnp.concatenate([qblk[:, :LANES], qblk[:, LANES:]], axis=0)
        s = lax.dot_general(kp_ref[2 * g + half], qg, (((1,), (1,)), ((), ())), preferred_element_type=F32)
        s_ref[...] = s
        m_ref[half] = jnp.max(s, axis=0, keepdims=True)

    def ex(u):
        p_ref[...] = jnp.exp2(s_ref[...] - m_ref[u % 2]).astype(BF16)

    def pv(u):
        q_rows, g, half = unit_index(u)
        v1t = jnp.concatenate([vt_ref[g], ones_rows], axis=0)
        o = _dot(v1t, p_ref[...])
        att_ref[half] = o[:HEAD_DIM] * (1.0 / o[HEAD_DIM:HEAD_DIM + 1])
        for c in range(2):
            pair = jnp.concatenate([att_ref[h, :, c * tq:(c + 1) * tq] for h in range(2)], axis=0)
            o_ref[g, q_rows, c * LANES:(c + 1) * LANES] = pair.T.astype(BF16)

    def step(u, carry):
        pv(u - 1)
        ex(u)
        qk(u + 1)
        return carry

    att_ref[...] = jnp.zeros(att_ref.shape, F32)
    qk(0)
    ex(0)
    qk(1)
    lax.fori_loop(1, n_units - 1, step, 0)
    pv(n_units - 2)
    ex(n_units - 1)
    pv(n_units - 1)


def _outproj_ffn2_kernel(x1_ref, f_ref, a_ref, wof_ref, woa_ref, g2_ref, wg_ref, wu_ref, wd_ref,
                         gf_ref, y_ref):
    for r in range(TOKEN_TILE // SUB_TILE):
        rows = slice(r * SUB_TILE, (r + 1) * SUB_TILE)
        x2 = x1_ref[rows, :] + _dot(f_ref[rows, :], wof_ref[...])
        for g in range(N_KV_HEADS):
            x2 = x2 + _dot(a_ref[g, rows, :], woa_ref[g])
        x3 = _swiglu_half_step(x2, g2_ref[...], wg_ref, wu_ref, wd_ref)
        y_ref[rows, :] = _rms_norm(x3, gf_ref[...])


def _resident(shape):
    return pl.BlockSpec(shape, lambda *_: (0,) * len(shape), pipeline_mode=pl.Buffered(1))


def _rope_tables(seq):
    rows = seq // GRID_W
    row_idx = jnp.repeat(jnp.arange(rows, dtype=F32), GRID_W)
    col_idx = jnp.tile(jnp.arange(GRID_W, dtype=F32), rows)
    inv_freq = ROPE_THETA ** (-jnp.arange(0, AXIS_DIM, 2, dtype=F32) / AXIS_DIM)
    ang_r = row_idx[:, None] * inv_freq[None, :]
    ang_c = col_idx[:, None] * inv_freq[None, :]
    cos_h = jnp.concatenate([jnp.cos(ang_r)] * 2 + [jnp.cos(ang_c)] * 2, axis=-1)
    sin_h = jnp.concatenate([-jnp.sin(ang_r), jnp.sin(ang_r), -jnp.sin(ang_c), jnp.sin(ang_c)], axis=-1)
    reps = LANES // HEAD_DIM
    return jnp.tile(cos_h, (1, reps)), jnp.tile(sin_h, (1, reps))


def _dft_cos_sin(n, rows, cols):
    j = jnp.arange(rows, dtype=jnp.int32)
    k = jnp.arange(cols, dtype=jnp.int32)
    ang = ((j[:, None] * k[None, :]) % n).astype(F32) * (2.0 * math.pi / n)
    return jnp.cos(ang), jnp.sin(ang)


def _pad_ff_cols(w):
    return jnp.pad(w, ((0, 0), (0, D_FF_PAD - D_FF))).astype(BF16)


def _pad_ff_rows(w):
    return jnp.pad(w, ((0, D_FF_PAD - D_FF), (0, 0))).astype(BF16)


def _row(v):
    return v.reshape(1, -1).astype(F32)


def kernel(x, ffn1_norm, ffn1_w_gate, ffn1_w_up, ffn1_w_down, mix_norm, w_in, fnet_w, fnet_b, q_norm,
           k_norm, w_out, ffn2_norm, ffn2_w_gate, ffn2_w_up, ffn2_w_down, final_norm):
    batch, seq, d = x.shape
    assert d == D_MODEL and seq % TOKEN_TILE == 0 and seq % ATTN_Q_TILE == 0 and seq % GRID_W == 0
    n_tok = batch * seq
    tiles_per_seq = seq // TOKEN_TILE
    x2d = x.reshape(n_tok, d)

    params = pltpu.CompilerParams(dimension_semantics=("arbitrary",), vmem_limit_bytes=VMEM_LIMIT_BYTES)
    tok = lambda width: pl.BlockSpec((TOKEN_TILE, width), lambda i: (i, 0))

    cos_t, sin_t = _rope_tables(seq)
    scale = math.log2(math.e) / math.sqrt(HEAD_DIM)
    gain_qk = jnp.concatenate([jnp.tile(q_norm.astype(F32) * scale, N_Q_HEADS),
                               jnp.tile(k_norm.astype(F32), N_KV_HEADS)]).reshape(1, QK_WIDTH)
    head_of_lane = jnp.arange(LANES) // HEAD_DIM
    seg_ones = (head_of_lane[:, None] == head_of_lane[None, :]).astype(BF16)
    rope_spec = pl.BlockSpec((TOKEN_TILE, LANES), lambda i: (i % tiles_per_seq, 0))

    grp = lambda n, width: pl.BlockSpec((n, TOKEN_TILE, width), lambda i: (0, i, 0))
    vt_spec = pl.BlockSpec((None, N_KV_HEADS, HEAD_DIM, TOKEN_TILE),
                           lambda i: (i // tiles_per_seq, 0, 0, i % tiles_per_seq))
    x1, uf, q, kp, vt = pl.pallas_call(
        _ffn1_inproj_kernel,
        grid=(n_tok // TOKEN_TILE,),
        in_specs=[tok(d), _resident((1, d)),
                  _resident((d, D_FF_PAD)), _resident((d, D_FF_PAD)), _resident((D_FF_PAD, d)),
                  _resident((1, d)), _resident((d, IN_WIDTH)), _resident((1, QK_WIDTH)),
                  rope_spec, rope_spec, _resident((LANES, LANES))],
        out_specs=[tok(d), tok(FNET_WIDTH), grp(N_KV_HEADS, 2 * LANES), grp(2 * N_KV_HEADS, LANES), vt_spec],
        out_shape=[jax.ShapeDtypeStruct((n_tok, d), F32),
                   jax.ShapeDtypeStruct((n_tok, FNET_WIDTH), BF16),
                   jax.ShapeDtypeStruct((N_KV_HEADS, n_tok, 2 * LANES), BF16),
                   jax.ShapeDtypeStruct((2 * N_KV_HEADS, n_tok, LANES), BF16),
                   jax.ShapeDtypeStruct((batch, N_KV_HEADS, HEAD_DIM, seq), BF16)],
        compiler_params=params,
        name="ffn1_inproj",
    )(x2d, _row(ffn1_norm), _pad_ff_cols(ffn1_w_gate), _pad_ff_cols(ffn1_w_up), _pad_ff_rows(ffn1_w_down),
      _row(mix_norm), w_in.astype(BF16), gain_qk, cos_t, sin_t, seg_ones)

    half = seq // 2
    dft_rows = half + BF16_SUBLANES
    cos_s, sin_s = _dft_cos_sin(seq, dft_rows, half)
    idx = jnp.arange(half, dtype=jnp.int32)
    rev = ((idx[:, None] + idx[None, :]) == half).astype(BF16)
    cos_c, sin_c = _dft_cos_sin(FNET_GROUP_DIM, FNET_GROUP_DIM, FNET_GROUP_DIM)
    ortho = 1.0 / math.sqrt(seq * FNET_GROUP_DIM)
    f_out = pl.pallas_call(
        _fourier_kernel,
        grid=(batch,),
        in_specs=[pl.BlockSpec((None, seq, FNET_WIDTH), lambda b: (b, 0, 0)),
                  _resident((dft_rows, half)), _resident((dft_rows, half)), _resident((half, half)),
                  _resident((FNET_GROUP_DIM, FNET_GROUP_DIM)), _resident((FNET_GROUP_DIM, FNET_GROUP_DIM)),
                  _resident((FNET_GROUPS, FNET_GROUP_DIM, FNET_GROUP_DIM)), _resident((1, FNET_WIDTH))],
        out_specs=pl.BlockSpec((None, seq, FNET_WIDTH), lambda b: (b, 0, 0)),
        out_shape=jax.ShapeDtypeStruct((batch, seq, FNET_WIDTH), BF16),
        scratch_shapes=[pltpu.VMEM((half, FNET_WIDTH), BF16), pltpu.VMEM((half, FNET_WIDTH), BF16)],
        compiler_params=params,
        name="fourier",
    )(uf.reshape(batch, seq, FNET_WIDTH), cos_s.astype(BF16), sin_s.astype(BF16), rev,
      cos_c * ortho, sin_c * ortho, fnet_w.astype(F32), fnet_b.reshape(1, FNET_WIDTH).astype(F32))

    a_out = pl.pallas_call(
        _attention_kernel,
        grid=(batch,),
        in_specs=[pl.BlockSpec((N_KV_HEADS, seq, 2 * LANES), lambda b: (0, b, 0)),
                  pl.BlockSpec((2 * N_KV_HEADS, seq, LANES), lambda b: (0, b, 0)),
                  pl.BlockSpec((None, N_KV_HEADS, HEAD_DIM, seq), lambda b: (b, 0, 0, 0))],
        out_specs=pl.BlockSpec((N_KV_HEADS, seq, 2 * LANES), lambda b: (0, b, 0)),
        out_shape=jax.ShapeDtypeStruct((N_KV_HEADS, n_tok, 2 * LANES), BF16),
        scratch_shapes=[pltpu.VMEM((seq, 2 * ATTN_Q_TILE), F32), pltpu.VMEM((seq, 2 * ATTN_Q_TILE), BF16),
                        pltpu.VMEM((2, 1, 2 * ATTN_Q_TILE), F32),
                        pltpu.VMEM((2, HEAD_DIM, 2 * ATTN_Q_TILE), F32)],
        compiler_params=params,
        name="attention",
    )(q, kp, vt)

    w_out_bf = w_out.astype(BF16)
    y = pl.pallas_call(
        _outproj_ffn2_kernel,
        grid=(n_tok // TOKEN_TILE,),
        in_specs=[tok(d), tok(FNET_WIDTH), grp(N_KV_HEADS, 2 * LANES),
                  _resident((FNET_WIDTH, d)), _resident((N_KV_HEADS, 2 * LANES, d)), _resident((1, d)),
                  _resident((d, D_FF_PAD)), _resident((d, D_FF_PAD)), _resident((D_FF_PAD, d)),
                  _resident((1, d))],
        out_specs=tok(d),
        out_shape=jax.ShapeDtypeStruct((n_tok, d), F32),
        compiler_params=params,
        name="outproj_ffn2",
    )(x1, f_out.reshape(n_tok, FNET_WIDTH), a_out,
      w_out_bf[:FNET_WIDTH], w_out_bf[FNET_WIDTH:].reshape(N_KV_HEADS, 2 * LANES, d), _row(ffn2_norm),
      _pad_ff_cols(ffn2_w_gate), _pad_ff_cols(ffn2_w_up), _pad_ff_rows(ffn2_w_down), _row(final_norm))
    return y.reshape(batch, seq, d)
```

```python
import functools
import math

import jax
import jax.numpy as jnp
from jax import lax
from jax.experimental import pallas as pl
from jax.experimental.pallas import tpu as pltpu

D_MODEL = 1024
D_FF = 2752
FNET_WIDTH = 512
FNET_GROUPS = 4
FNET_GROUP_DIM = 128
HEAD_DIM = 64
N_Q_HEADS = 8
N_KV_HEADS = 2
GQA_GROUP = N_Q_HEADS // N_KV_HEADS
ATTN_WIDTH = N_Q_HEADS * HEAD_DIM
KV_WIDTH = N_KV_HEADS * HEAD_DIM
QK_WIDTH = ATTN_WIDTH + KV_WIDTH
KP_WIDTH = 2 * N_KV_HEADS * 128
IN_WIDTH = FNET_WIDTH + ATTN_WIDTH + 2 * KV_WIDTH
GRID_W = 64
AXIS_DIM = HEAD_DIM // 2
ROPE_THETA = 10000.0
EPS = 1e-6

LANES = 128
BF16_SUBLANES = 16
MXU_DIM = 256
VMEM_LIMIT_BYTES = 60000 * 1024

D_FF_PAD = ((D_FF + MXU_DIM - 1) // MXU_DIM) * MXU_DIM

TOKEN_TILE = 1024
SUB_TILE = 256
ATTN_Q_TILE = 256

BF16 = jnp.bfloat16
F32 = jnp.float32


def _rms_norm(x, gain):
    return x * lax.rsqrt(jnp.mean(x * x, axis=-1, keepdims=True) + EPS) * gain


def _dot(a, b):
    return jnp.dot(a, b, preferred_element_type=F32)


def _swiglu_half_step(x, gain, wg_ref, wu_ref, wd_ref):
    xn = _rms_norm(x, gain).astype(BF16)
    g = _dot(xn, wg_ref[...])
    u = _dot(xn, wu_ref[...])
    a = (g * jax.nn.sigmoid(g) * u).astype(BF16)
    return x + 0.5 * _dot(a, wd_ref[...])


def _split_hi_lo(x):
    hi = x.astype(BF16)
    lo = (x - hi.astype(F32)).astype(BF16)
    return hi, lo


def _ffn1_inproj_kernel(x_ref, g1_ref, wg_ref, wu_ref, wd_ref, gm_ref, win_ref, gqk_ref,
                        cos_ref, sin_ref, seg_ref,
                        x1_ref, uf_ref, q_ref, kp_ref, vt_ref):
    seg = seg_ref[...]
    lane = lax.broadcasted_iota(jnp.int32, (1, LANES), 1)
    upper_half = (lane & (AXIS_DIM // 2)) != 0

    def gate_up(rows):
        xn = _rms_norm(x_ref[rows, :], g1_ref[...]).astype(BF16)
        g = _dot(xn, wg_ref[...])
        return (g * jax.nn.sigmoid(g) * _dot(xn, wu_ref[...])).astype(BF16)

    def down(rows, act):
        x1_ref[rows, :] = x_ref[rows, :] + 0.5 * _dot(act, wd_ref[...])

    def in_proj(rows):
        h = _rms_norm(x1_ref[rows, :], gm_ref[...]).astype(BF16)
        return _dot(h, win_ref[...])

    def heads(rows, u):
        uf_ref[rows, :] = u[:, :FNET_WIDTH].astype(BF16)
        vt = u[:, FNET_WIDTH + QK_WIDTH:].T.astype(BF16)
        for g in range(N_KV_HEADS):
            vt_ref[g, :, rows] = vt[g * HEAD_DIM:(g + 1) * HEAD_DIM]

        cos_t = cos_ref[rows, :]
        sin_t = sin_ref[rows, :]
        for c in range(QK_WIDTH // LANES):
            lo_col = FNET_WIDTH + c * LANES
            z = u[:, lo_col:lo_col + LANES]
            hi, lo = _split_hi_lo(z * z)
            mean_sq = (_dot(hi, seg) + _dot(lo, seg)) * (1.0 / HEAD_DIM)
            zn = z * lax.rsqrt(mean_sq + EPS) * gqk_ref[:, c * LANES:(c + 1) * LANES]
            partner = jnp.where(upper_half,
                                pltpu.roll(zn, AXIS_DIM // 2, axis=1),
                                pltpu.roll(zn, LANES - AXIS_DIM // 2, axis=1))
            out = zn * cos_t + partner * sin_t
            if c < ATTN_WIDTH // LANES:
                q_ref[c // 2, rows, (c % 2) * LANES:(c % 2 + 1) * LANES] = out.astype(BF16)
            else:
                swapped = pltpu.roll(out, HEAD_DIM, axis=1)
                first = lane < HEAD_DIM
                variants = (jnp.where(first, out, 0.0), jnp.where(first, 0.0, swapped),
                            jnp.where(first, swapped, 0.0), jnp.where(first, 0.0, out))
                for i, kv in enumerate(variants):
                    kp_ref[i, rows, :] = kv.astype(BF16)

    sub_tiles = [slice(r * SUB_TILE, (r + 1) * SUB_TILE) for r in range(TOKEN_TILE // SUB_TILE)]
    for slot in range(len(sub_tiles) + 1):
        cur = sub_tiles[slot] if slot < len(sub_tiles) else None
        prev = sub_tiles[slot - 1] if slot > 0 else None
        if cur is not None:
            act = gate_up(cur)
        if prev is not None:
            u_prev = in_proj(prev)
        if cur is not None:
            down(cur, act)
        if prev is not None:
            heads(prev, u_prev)


def _fourier_kernel(uf_ref, ch_ref, sh_ref, rev_ref, cc_ref, sc_ref, fw_ref, fb_ref, out_ref, e1_ref, o2_ref):
    n = uf_ref.shape[0]
    h = n // 2
    lo = uf_ref[:h, :]
    hi = uf_ref[h:, :]
    rev = rev_ref[...]
    mirrored = _dot(rev, hi)
    lo32 = lo.astype(F32)
    even = (lo32 + mirrored).astype(BF16)
    odd = (lo32 - mirrored).astype(BF16)

    cc_hi, cc_lo = _split_hi_lo(cc_ref[...])
    sc_hi, sc_lo = _split_hi_lo(sc_ref[...])
    mid_rows = []
    for g in range(FNET_GROUPS):
        cols = slice(g * FNET_GROUP_DIM, (g + 1) * FNET_GROUP_DIM)
        w_hi, w_lo = _split_hi_lo(fw_ref[g])
        wc = (_dot(cc_hi, w_hi) + _dot(cc_hi, w_lo) + _dot(cc_lo, w_hi)).astype(BF16)
        ws = (_dot(sc_hi, w_hi) + _dot(sc_hi, w_lo) + _dot(sc_lo, w_hi)).astype(BF16)
        e1_ref[:, cols] = _dot(even[:, cols], wc).astype(BF16)
        o2_ref[:, cols] = _dot(odd[:, cols], ws).astype(BF16)
        mid_rows.append(_dot(hi[:BF16_SUBLANES, cols], wc)[0:1])
    mid = jnp.concatenate(mid_rows, axis=-1)

    a = _dot(ch_ref[...], e1_ref[...])
    b = _dot(sh_ref[...], o2_ref[...])
    j = lax.broadcasted_iota(jnp.int32, (a.shape[0], 1), 0)
    a = a + jnp.where((j & 1) == 0, 1.0, -1.0) * mid + fb_ref[...]
    out_ref[:h, :] = (a[:h] - b[:h]).astype(BF16)
    z = (a + b).astype(BF16)
    out_ref[h:, :] = _dot(rev, z[:h]).astype(BF16)
    out_ref[h:h + 1, :] = z[h:h + 1]


def _attention_kernel(q_ref, kp_ref, vt_ref, o_ref, s_ref, p_ref, m_ref, att_ref):
    seq = kp_ref.shape[1]
    tq = ATTN_Q_TILE
    units_per_tile = 2 * N_KV_HEADS
    n_units = (seq // tq) * units_per_tile
    ones_rows = jnp.ones((BF16_SUBLANES, seq), BF16)

    def unit_index(u):
        t, r = u // units_per_tile, u % units_per_tile
        start = t * tq
        if not isinstance(start, int):
            start = pl.multiple_of(start, tq)
        return pl.ds(start, tq), r // 2, r % 2

    def qk(u):
        q_rows, g, half = unit_index(u)
        qblk = q_ref[g, q_rows, :]
        qg = jnp.concatenate([qblk[:, :LANES], qblk[:, LANES:]], axis=0)
        s = lax.dot_general(kp_ref[2 * g + half], qg, (((1,), (1,)), ((), ())), preferred_element_type=F32)
        s_ref[...] = s
        m_ref[half] = jnp.max(s, axis=0, keepdims=True)

    def ex(u):
        p_ref[...] = jnp.exp2(s_ref[...] - m_ref[u % 2]).astype(BF16)

    def pv(u):
        q_rows, g, half = unit_index(u)
        v1t = jnp.concatenate([vt_ref[g], ones_rows], axis=0)
        o = _dot(v1t, p_ref[...])
        att_ref[half] = o[:HEAD_DIM] * (1.0 / o[HEAD_DIM:HEAD_DIM + 1])
        for c in range(2):
            pair = jnp.concatenate([att_ref[h, :, c * tq:(c + 1) * tq] for h in range(2)], axis=0)
            o_ref[g, q_rows, c * LANES:(c + 1) * LANES] = pair.T.astype(BF16)

    def step(u, carry):
        pv(u - 1)
        ex(u)
        qk(u + 1)
        return carry

    att_ref[...] = jnp.zeros(att_ref.shape, F32)
    qk(0)
    ex(0)
    qk(1)
    lax.fori_loop(1, n_units - 1, step, 0)
    pv(n_units - 2)
    ex(n_units - 1)
    pv(n_units - 1)


def _outproj_ffn2_kernel(x1_ref, f_ref, a_ref, wof_ref, woa_ref, g2_ref, wg_ref, wu_ref, wd_ref,
                         gf_ref, y_ref):
    sub_tiles = [slice(r * SUB_TILE, (r + 1) * SUB_TILE) for r in range(TOKEN_TILE // SUB_TILE)]
    for rows in sub_tiles:
        x2 = x1_ref[rows, :] + _dot(f_ref[rows, :], wof_ref[...])
        for g in range(N_KV_HEADS):
            x2 = x2 + _dot(a_ref[g, rows, :], woa_ref[g])
        y_ref[rows, :] = x2
    for rows in sub_tiles:
        x3 = _swiglu_half_step(y_ref[rows, :], g2_ref[...], wg_ref, wu_ref, wd_ref)
        y_ref[rows, :] = _rms_norm(x3, gf_ref[...])


def _resident(shape):
    return pl.BlockSpec(shape, lambda *_: (0,) * len(shape), pipeline_mode=pl.Buffered(1))


def _rope_tables(seq):
    rows = seq // GRID_W
    row_idx = jnp.repeat(jnp.arange(rows, dtype=F32), GRID_W)
    col_idx = jnp.tile(jnp.arange(GRID_W, dtype=F32), rows)
    inv_freq = ROPE_THETA ** (-jnp.arange(0, AXIS_DIM, 2, dtype=F32) / AXIS_DIM)
    ang_r = row_idx[:, None] * inv_freq[None, :]
    ang_c = col_idx[:, None] * inv_freq[None, :]
    cos_h = jnp.concatenate([jnp.cos(ang_r)] * 2 + [jnp.cos(ang_c)] * 2, axis=-1)
    sin_h = jnp.concatenate([-jnp.sin(ang_r), jnp.sin(ang_r), -jnp.sin(ang_c), jnp.sin(ang_c)], axis=-1)
    reps = LANES // HEAD_DIM
    return jnp.tile(cos_h, (1, reps)), jnp.tile(sin_h, (1, reps))


def _dft_cos_sin(n, rows, cols):
    j = jnp.arange(rows, dtype=jnp.int32)
    k = jnp.arange(cols, dtype=jnp.int32)
    ang = ((j[:, None] * k[None, :]) % n).astype(F32) * (2.0 * math.pi / n)
    return jnp.cos(ang), jnp.sin(ang)


def _pad_ff_cols(w):
    return jnp.pad(w, ((0, 0), (0, D_FF_PAD - D_FF))).astype(BF16)


def _pad_ff_rows(w):
    return jnp.pad(w, ((0, D_FF_PAD - D_FF), (0, 0))).astype(BF16)


def _row(v):
    return v.reshape(1, -1).astype(F32)


def kernel(x, ffn1_norm, ffn1_w_gate, ffn1_w_up, ffn1_w_down, mix_norm, w_in, fnet_w, fnet_b, q_norm,
           k_norm, w_out, ffn2_norm, ffn2_w_gate, ffn2_w_up, ffn2_w_down, final_norm):
    batch, seq, d = x.shape
    assert d == D_MODEL and seq % TOKEN_TILE == 0 and seq % ATTN_Q_TILE == 0 and seq % GRID_W == 0
    n_tok = batch * seq
    tiles_per_seq = seq // TOKEN_TILE
    x2d = x.reshape(n_tok, d)

    params = pltpu.CompilerParams(dimension_semantics=("arbitrary",), vmem_limit_bytes=VMEM_LIMIT_BYTES)
    tok = lambda width: pl.BlockSpec((TOKEN_TILE, width), lambda i: (i, 0))

    cos_t, sin_t = _rope_tables(seq)
    scale = math.log2(math.e) / math.sqrt(HEAD_DIM)
    gain_qk = jnp.concatenate([jnp.tile(q_norm.astype(F32) * scale, N_Q_HEADS),
                               jnp.tile(k_norm.astype(F32), N_KV_HEADS)]).reshape(1, QK_WIDTH)
    head_of_lane = jnp.arange(LANES) // HEAD_DIM
    seg_ones = (head_of_lane[:, None] == head_of_lane[None, :]).astype(BF16)
    rope_spec = pl.BlockSpec((TOKEN_TILE, LANES), lambda i: (i % tiles_per_seq, 0))

    grp = lambda n, width: pl.BlockSpec((n, TOKEN_TILE, width), lambda i: (0, i, 0))
    vt_spec = pl.BlockSpec((None, N_KV_HEADS, HEAD_DIM, TOKEN_TILE),
                           lambda i: (i // tiles_per_seq, 0, 0, i % tiles_per_seq))
    x1, uf, q, kp, vt = pl.pallas_call(
        _ffn1_inproj_kernel,
        grid=(n_tok // TOKEN_TILE,),
        in_specs=[tok(d), _resident((1, d)),
                  _resident((d, D_FF_PAD)), _resident((d, D_FF_PAD)), _resident((D_FF_PAD, d)),
                  _resident((1, d)), _resident((d, IN_WIDTH)), _resident((1, QK_WIDTH)),
                  rope_spec, rope_spec, _resident((LANES, LANES))],
        out_specs=[tok(d), tok(FNET_WIDTH), grp(N_KV_HEADS, 2 * LANES), grp(2 * N_KV_HEADS, LANES), vt_spec],
        out_shape=[jax.ShapeDtypeStruct((n_tok, d), F32),
                   jax.ShapeDtypeStruct((n_tok, FNET_WIDTH), BF16),
                   jax.ShapeDtypeStruct((N_KV_HEADS, n_tok, 2 * LANES), BF16),
                   jax.ShapeDtypeStruct((2 * N_KV_HEADS, n_tok, LANES), BF16),
                   jax.ShapeDtypeStruct((batch, N_KV_HEADS, HEAD_DIM, seq), BF16)],
        compiler_params=params,
        name="ffn1_inproj",
    )(x2d, _row(ffn1_norm), _pad_ff_cols(ffn1_w_gate), _pad_ff_cols(ffn1_w_up), _pad_ff_rows(ffn1_w_down),
      _row(mix_norm), w_in.astype(BF16), gain_qk, cos_t, sin_t, seg_ones)

    half = seq // 2
    dft_rows = half + BF16_SUBLANES
    cos_s, sin_s = _dft_cos_sin(seq, dft_rows, half)
    idx = jnp.arange(half, dtype=jnp.int32)
    rev = ((idx[:, None] + idx[None, :]) == half).astype(BF16)
    cos_c, sin_c = _dft_cos_sin(FNET_GROUP_DIM, FNET_GROUP_DIM, FNET_GROUP_DIM)
    ortho = 1.0 / math.sqrt(seq * FNET_GROUP_DIM)
    f_out = pl.pallas_call(
        _fourier_kernel,
        grid=(batch,),
        in_specs=[pl.BlockSpec((None, seq, FNET_WIDTH), lambda b: (b, 0, 0)),
                  _resident((dft_rows, half)), _resident((dft_rows, half)), _resident((half, half)),
                  _resident((FNET_GROUP_DIM, FNET_GROUP_DIM)), _resident((FNET_GROUP_DIM, FNET_GROUP_DIM)),
                  _resident((FNET_GROUPS, FNET_GROUP_DIM, FNET_GROUP_DIM)), _resident((1, FNET_WIDTH))],
        out_specs=pl.BlockSpec((None, seq, FNET_WIDTH), lambda b: (b, 0, 0)),
        out_shape=jax.ShapeDtypeStruct((batch, seq, FNET_WIDTH), BF16),
        scratch_shapes=[pltpu.VMEM((half, FNET_WIDTH), BF16), pltpu.VMEM((half, FNET_WIDTH), BF16)],
        compiler_params=params,
        name="fourier",
    )(uf.reshape(batch, seq, FNET_WIDTH), cos_s.astype(BF16), sin_s.astype(BF16), rev,
      cos_c * ortho, sin_c * ortho, fnet_w.astype(F32), fnet_b.reshape(1, FNET_WIDTH).astype(F32))

    a_out = pl.pallas_call(
        _attention_kernel,
        grid=(batch,),
        in_specs=[pl.BlockSpec((N_KV_HEADS, seq, 2 * LANES), lambda b: (0, b, 0)),
                  pl.BlockSpec((2 * N_KV_HEADS, seq, LANES), lambda b: (0, b, 0)),
                  pl.BlockSpec((None, N_KV_HEADS, HEAD_DIM, seq), lambda b: (b, 0, 0, 0))],
        out_specs=pl.BlockSpec((N_KV_HEADS, seq, 2 * LANES), lambda b: (0, b, 0)),
        out_shape=jax.ShapeDtypeStruct((N_KV_HEADS, n_tok, 2 * LANES), BF16),
        scratch_shapes=[pltpu.VMEM((seq, 2 * ATTN_Q_TILE), F32), pltpu.VMEM((seq, 2 * ATTN_Q_TILE), BF16),
                        pltpu.VMEM((2, 1, 2 * ATTN_Q_TILE), F32),
                        pltpu.VMEM((2, HEAD_DIM, 2 * ATTN_Q_TILE), F32)],
        compiler_params=params,
        name="attention",
    )(q, kp, vt)

    w_out_bf = w_out.astype(BF16)
    y = pl.pallas_call(
        _outproj_ffn2_kernel,
        grid=(n_tok // TOKEN_TILE,),
        in_specs=[tok(d), tok(FNET_WIDTH), grp(N_KV_HEADS, 2 * LANES),
                  _resident((FNET_WIDTH, d)), _resident((N_KV_HEADS, 2 * LANES, d)), _resident((1, d)),
                  _resident((d, D_FF_PAD)), _resident((d, D_FF_PAD)), _resident((D_FF_PAD, d)),
                  _resident((1, d))],
        out_specs=tok(d),
        out_shape=jax.ShapeDtypeStruct((n_tok, d), F32),
        compiler_params=params,
        name="outproj_ffn2",
    )(x1, f_out.reshape(n_tok, FNET_WIDTH), a_out,
      w_out_bf[:FNET_WIDTH], w_out_bf[FNET_WIDTH:].reshape(N_KV_HEADS, 2 * LANES, d), _row(ffn2_norm),
      _pad_ff_cols(ffn2_w_gate), _pad_ff_cols(ffn2_w_up), _pad_ff_rows(ffn2_w_down), _row(final_norm))
    return y.reshape(batch, seq, d)
```

```python
import math

import numpy as np
import jax
import jax.numpy as jnp
from jax import lax
from jax.experimental import pallas as pl
from jax.experimental.pallas import tpu as pltpu

D_MODEL = 1024
D_FF = 2752
FNET_WIDTH = 512
FNET_GROUPS = 4
FNET_GROUP_DIM = 128
HEAD_DIM = 64
N_Q_HEADS = 8
N_KV_HEADS = 2
GQA_GROUP = N_Q_HEADS // N_KV_HEADS
ATTN_WIDTH = N_Q_HEADS * HEAD_DIM
KV_WIDTH = N_KV_HEADS * HEAD_DIM
QK_WIDTH = ATTN_WIDTH + KV_WIDTH
KP_WIDTH = 2 * N_KV_HEADS * 128
IN_WIDTH = FNET_WIDTH + ATTN_WIDTH + 2 * KV_WIDTH
GRID_W = 64
AXIS_DIM = HEAD_DIM // 2
ROPE_THETA = 10000.0
EPS = 1e-6

LANES = 128
BF16_SUBLANES = 16
MXU_DIM = 256
VMEM_LIMIT_BYTES = 60000 * 1024

D_FF_PAD = ((D_FF + MXU_DIM - 1) // MXU_DIM) * MXU_DIM

TOKEN_TILE = 1024
SUB_TILE = 256
ATTN_Q_TILE = 256
WEIGHT_CAST_STEPS = 4

BF16 = jnp.bfloat16
F32 = jnp.float32


def _rms_norm(x, gain):
    return x * lax.rsqrt(jnp.mean(x * x, axis=-1, keepdims=True) + EPS) * gain


def _dot(a, b):
    return jnp.dot(a, b, preferred_element_type=F32)


def _swiglu_half_step(x, gain, wg_ref, wu_ref, wd_ref):
    xn = _rms_norm(x, gain).astype(BF16)
    g = _dot(xn, wg_ref[...])
    u = _dot(xn, wu_ref[...])
    a = (g * jax.nn.sigmoid(g) * u).astype(BF16)
    return x + 0.5 * _dot(a, wd_ref[...])


def _split_hi_lo(x):
    hi = x.astype(BF16)
    lo = (x - hi.astype(F32)).astype(BF16)
    return hi, lo


def _ffn1_inproj_kernel(x_ref, g1_ref, wg_ref, wu_ref, wd_ref, gm_ref, win_ref, gqk_ref,
                        cos_ref, sin_ref, seg_ref,
                        x1_ref, uf_ref, q_ref, kp_ref, vt_ref):
    seg = seg_ref[...]
    lane = lax.broadcasted_iota(jnp.int32, (1, LANES), 1)
    upper_half = (lane & (AXIS_DIM // 2)) != 0

    def gate_up(rows):
        xn = _rms_norm(x_ref[rows, :], g1_ref[...]).astype(BF16)
        g = _dot(xn, wg_ref[...])
        return (g * jax.nn.sigmoid(g) * _dot(xn, wu_ref[...])).astype(BF16)

    def down(rows, act):
        x1_ref[rows, :] = x_ref[rows, :] + 0.5 * _dot(act, wd_ref[...])

    def in_proj(rows):
        h = _rms_norm(x1_ref[rows, :], gm_ref[...]).astype(BF16)
        return _dot(h, win_ref[...])

    def heads(rows, u):
        uf_ref[rows, :] = u[:, :FNET_WIDTH].astype(BF16)
        vt = u[:, FNET_WIDTH + QK_WIDTH:].T.astype(BF16)
        for g in range(N_KV_HEADS):
            vt_ref[g, :, rows] = vt[g * HEAD_DIM:(g + 1) * HEAD_DIM]

        cos_t = cos_ref[rows, :]
        sin_t = sin_ref[rows, :]
        for c in range(QK_WIDTH // LANES):
            lo_col = FNET_WIDTH + c * LANES
            z = u[:, lo_col:lo_col + LANES]
            hi, lo = _split_hi_lo(z * z)
            mean_sq = (_dot(hi, seg) + _dot(lo, seg)) * (1.0 / HEAD_DIM)
            zn = z * lax.rsqrt(mean_sq + EPS) * gqk_ref[:, c * LANES:(c + 1) * LANES]
            partner = jnp.where(upper_half,
                                pltpu.roll(zn, AXIS_DIM // 2, axis=1),
                                pltpu.roll(zn, LANES - AXIS_DIM // 2, axis=1))
            out = zn * cos_t + partner * sin_t
            if c < ATTN_WIDTH // LANES:
                q_ref[c // 2, rows, (c % 2) * LANES:(c % 2 + 1) * LANES] = out.astype(BF16)
            else:
                swapped = pltpu.roll(out, HEAD_DIM, axis=1)
                first = lane < HEAD_DIM
                variants = (jnp.where(first, out, 0.0), jnp.where(first, 0.0, swapped),
                            jnp.where(first, swapped, 0.0), jnp.where(first, 0.0, out))
                for i, kv in enumerate(variants):
                    kp_ref[i, rows, :] = kv.astype(BF16)

    sub_tiles = [slice(r * SUB_TILE, (r + 1) * SUB_TILE) for r in range(TOKEN_TILE // SUB_TILE)]
    for slot in range(len(sub_tiles) + 1):
        cur = sub_tiles[slot] if slot < len(sub_tiles) else None
        prev = sub_tiles[slot - 1] if slot > 0 else None
        if cur is not None:
            act = gate_up(cur)
        if prev is not None:
            u_prev = in_proj(prev)
        if cur is not None:
            down(cur, act)
        if prev is not None:
            heads(prev, u_prev)


def _fourier_kernel(uf_ref, ch_ref, sh_ref, rev_ref, cc_ref, sc_ref, fw_ref, fb_ref, out_ref, e1_ref, o2_ref):
    n = uf_ref.shape[0]
    h = n // 2
    lo = uf_ref[:h, :]
    hi = uf_ref[h:, :]
    rev = rev_ref[...]
    mirrored = _dot(rev, hi)
    lo32 = lo.astype(F32)
    even = (lo32 + mirrored).astype(BF16)
    odd = (lo32 - mirrored).astype(BF16)

    cc_hi, cc_lo = _split_hi_lo(cc_ref[...])
    sc_hi, sc_lo = _split_hi_lo(sc_ref[...])
    mid_rows = []
    for g in range(FNET_GROUPS):
        cols = slice(g * FNET_GROUP_DIM, (g + 1) * FNET_GROUP_DIM)
        w_hi, w_lo = _split_hi_lo(fw_ref[g])
        wc = (_dot(cc_hi, w_hi) + _dot(cc_hi, w_lo) + _dot(cc_lo, w_hi)).astype(BF16)
        ws = (_dot(sc_hi, w_hi) + _dot(sc_hi, w_lo) + _dot(sc_lo, w_hi)).astype(BF16)
        e1_ref[:, cols] = _dot(even[:, cols], wc).astype(BF16)
        o2_ref[:, cols] = _dot(odd[:, cols], ws).astype(BF16)
        mid_rows.append(_dot(hi[:BF16_SUBLANES, cols], wc)[0:1])
    mid = jnp.concatenate(mid_rows, axis=-1)

    a = _dot(ch_ref[...], e1_ref[...])
    b = _dot(sh_ref[...], o2_ref[...])
    j = lax.broadcasted_iota(jnp.int32, (a.shape[0], 1), 0)
    a = a + jnp.where((j & 1) == 0, 1.0, -1.0) * mid + fb_ref[...]
    out_ref[:h, :] = (a[:h] - b[:h]).astype(BF16)
    z = (a + b).astype(BF16)
    out_ref[h:, :] = _dot(rev, z[:h]).astype(BF16)
    out_ref[h:h + 1, :] = z[h:h + 1]


def _attention_kernel(q_ref, kp_ref, vt_ref, o_ref, s_ref, p_ref, m_ref, att_ref):
    seq = kp_ref.shape[1]
    tq = ATTN_Q_TILE
    units_per_tile = 2 * N_KV_HEADS
    n_units = (seq // tq) * units_per_tile
    ones_rows = jnp.ones((BF16_SUBLANES, seq), BF16)

    def unit_index(u):
        t, r = u // units_per_tile, u % units_per_tile
        start = t * tq
        if not isinstance(start, int):
            start = pl.multiple_of(start, tq)
        return pl.ds(start, tq), r // 2, r % 2

    def qk(u):
        q_rows, g, half = unit_index(u)
        qblk = q_ref[g, q_rows, :]
        qg = jnp.concatenate([qblk[:, :LANES], qblk[:, LANES:]], axis=0)
        s = lax.dot_general(kp_ref[2 * g + half], qg, (((1,), (1,)), ((), ())), preferred_element_type=F32)
        s_ref[...] = s
        m_ref[half] = jnp.max(s, axis=0, keepdims=True)

    def ex(u):
        p_ref[...] = jnp.exp2(s_ref[...] - m_ref[u % 2]).astype(BF16)

    def pv(u):
        q_rows, g, half = unit_index(u)
        v1t = jnp.concatenate([vt_ref[g], ones_rows], axis=0)
        o = _dot(v1t, p_ref[...])
        att_ref[half] = o[:HEAD_DIM] * (1.0 / o[HEAD_DIM:HEAD_DIM + 1])
        for c in range(2):
            pair = jnp.concatenate([att_ref[h, :, c * tq:(c + 1) * tq] for h in range(2)], axis=0)
            o_ref[g, q_rows, c * LANES:(c + 1) * LANES] = pair.T.astype(BF16)

    def step(u, carry):
        pv(u - 1)
        ex(u)
        qk(u + 1)
        return carry

    att_ref[...] = jnp.zeros(att_ref.shape, F32)
    qk(0)
    ex(0)
    qk(1)
    lax.fori_loop(1, n_units - 1, step, 0)
    pv(n_units - 2)
    ex(n_units - 1)
    pv(n_units - 1)


def _outproj_ffn2_kernel(x1_ref, f_ref, a_ref, wof_ref, woa_ref, g2_ref, wg_ref, wu_ref, wd_ref,
                         gf_ref, y_ref):
    sub_tiles = [slice(r * SUB_TILE, (r + 1) * SUB_TILE) for r in range(TOKEN_TILE // SUB_TILE)]
    for rows in sub_tiles:
        x2 = x1_ref[rows, :] + _dot(f_ref[rows, :], wof_ref[...])
        for g in range(N_KV_HEADS):
            x2 = x2 + _dot(a_ref[g, rows, :], woa_ref[g])
        y_ref[rows, :] = x2
    for rows in sub_tiles:
        x3 = _swiglu_half_step(y_ref[rows, :], g2_ref[...], wg_ref, wu_ref, wd_ref)
        y_ref[rows, :] = _rms_norm(x3, gf_ref[...])


def _resident(shape):
    return pl.BlockSpec(shape, lambda *_: (0,) * len(shape), pipeline_mode=pl.Buffered(1))


def _rope_tables(seq):
    rows = seq // GRID_W
    row_idx = np.repeat(np.arange(rows, dtype=np.float64), GRID_W)
    col_idx = np.tile(np.arange(GRID_W, dtype=np.float64), rows)
    inv_freq = ROPE_THETA ** (-np.arange(0, AXIS_DIM, 2, dtype=np.float64) / AXIS_DIM)
    ang_r = row_idx[:, None] * inv_freq[None, :]
    ang_c = col_idx[:, None] * inv_freq[None, :]
    cos_h = np.concatenate([np.cos(ang_r)] * 2 + [np.cos(ang_c)] * 2, axis=-1)
    sin_h = np.concatenate([-np.sin(ang_r), np.sin(ang_r), -np.sin(ang_c), np.sin(ang_c)], axis=-1)
    reps = LANES // HEAD_DIM
    return (np.tile(cos_h, (1, reps)).astype(np.float32), np.tile(sin_h, (1, reps)).astype(np.float32))


def _dft_cos_sin(n, rows, cols):
    j = np.arange(rows, dtype=np.int64)
    k = np.arange(cols, dtype=np.int64)
    ang = ((j[:, None] * k[None, :]) % n).astype(np.float64) * (2.0 * math.pi / n)
    return np.cos(ang).astype(np.float32), np.sin(ang).astype(np.float32)


def _cast_pad_cols_kernel(*refs):
    n = len(refs) // 2
    for w_ref, o_ref in zip(refs[:n], refs[n:]):
        o_ref[:, :D_FF] = w_ref[...].astype(BF16)
        o_ref[:, D_FF:] = jnp.zeros((o_ref.shape[0], D_FF_PAD - D_FF), BF16)


def _cast_pad_rows_kernel(*refs):
    n = len(refs) // 2
    for w_ref, o_ref in zip(refs[:n], refs[n:]):
        o_ref[:D_FF, :] = w_ref[...].astype(BF16)
        o_ref[D_FF:, :] = jnp.zeros((D_FF_PAD - D_FF, o_ref.shape[1]), BF16)


def _cast_kernel(*refs):
    n = len(refs) // 2
    for w_ref, o_ref in zip(refs[:n], refs[n:]):
        o_ref[...] = w_ref[...].astype(BF16)


def _cast_weights(kernel_fn, weights, in_block, out_shape, out_block, index_map, steps, name):
    return pl.pallas_call(
        kernel_fn,
        grid=(steps,),
        in_specs=[pl.BlockSpec(in_block, index_map)] * len(weights),
        out_specs=[pl.BlockSpec(out_block, index_map)] * len(weights),
        out_shape=[jax.ShapeDtypeStruct(out_shape, BF16)] * len(weights),
        compiler_params=pltpu.CompilerParams(dimension_semantics=("arbitrary",),
                                             vmem_limit_bytes=VMEM_LIMIT_BYTES),
        name=name,
    )(*weights)


def _row(v):
    return v.reshape(1, -1).astype(F32)


def kernel(x, ffn1_norm, ffn1_w_gate, ffn1_w_up, ffn1_w_down, mix_norm, w_in, fnet_w, fnet_b, q_norm,
           k_norm, w_out, ffn2_norm, ffn2_w_gate, ffn2_w_up, ffn2_w_down, final_norm):
    batch, seq, d = x.shape
    assert d == D_MODEL and seq % TOKEN_TILE == 0 and seq % ATTN_Q_TILE == 0 and seq % GRID_W == 0
    n_tok = batch * seq
    tiles_per_seq = seq // TOKEN_TILE
    x2d = x.reshape(n_tok, d)

    params = pltpu.CompilerParams(dimension_semantics=("arbitrary",), vmem_limit_bytes=VMEM_LIMIT_BYTES)
    tok = lambda width: pl.BlockSpec((TOKEN_TILE, width), lambda i: (i, 0))

    blk = d // WEIGHT_CAST_STEPS
    by_rows, by_cols = (lambda i: (i, 0)), (lambda i: (0, i))
    wg1, wu1, wg2, wu2 = _cast_weights(
        _cast_pad_cols_kernel, [ffn1_w_gate, ffn1_w_up, ffn2_w_gate, ffn2_w_up],
        (blk, D_FF), (d, D_FF_PAD), (blk, D_FF_PAD), by_rows, WEIGHT_CAST_STEPS, "cast_gate_up")
    wd1, wd2 = _cast_weights(
        _cast_pad_rows_kernel, [ffn1_w_down, ffn2_w_down],
        (D_FF, blk), (D_FF_PAD, d), (D_FF_PAD, blk), by_cols, WEIGHT_CAST_STEPS, "cast_down")
    w_in_bf, = _cast_weights(_cast_kernel, [w_in], (blk, IN_WIDTH), (d, IN_WIDTH), (blk, IN_WIDTH), by_rows,
                             WEIGHT_CAST_STEPS, "cast_w_in")
    w_out_bf, = _cast_weights(_cast_kernel, [w_out], (blk, d), (d, d), (blk, d), by_rows,
                              WEIGHT_CAST_STEPS, "cast_w_out")

    cos_t, sin_t = _rope_tables(seq)
    scale = math.log2(math.e) / math.sqrt(HEAD_DIM)
    gain_qk = jnp.concatenate([jnp.tile(q_norm.astype(F32) * scale, N_Q_HEADS),
                               jnp.tile(k_norm.astype(F32), N_KV_HEADS)]).reshape(1, QK_WIDTH)
    head_of_lane = jnp.arange(LANES) // HEAD_DIM
    seg_ones = (head_of_lane[:, None] == head_of_lane[None, :]).astype(BF16)
    rope_spec = pl.BlockSpec((TOKEN_TILE, LANES), lambda i: (i % tiles_per_seq, 0))

    grp = lambda n, width: pl.BlockSpec((n, TOKEN_TILE, width), lambda i: (0, i, 0))
    vt_spec = pl.BlockSpec((None, N_KV_HEADS, HEAD_DIM, TOKEN_TILE),
                           lambda i: (i // tiles_per_seq, 0, 0, i % tiles_per_seq))
    x1, uf, q, kp, vt = pl.pallas_call(
        _ffn1_inproj_kernel,
        grid=(n_tok // TOKEN_TILE,),
        in_specs=[tok(d), _resident((1, d)),
                  _resident((d, D_FF_PAD)), _resident((d, D_FF_PAD)), _resident((D_FF_PAD, d)),
                  _resident((1, d)), _resident((d, IN_WIDTH)), _resident((1, QK_WIDTH)),
                  rope_spec, rope_spec, _resident((LANES, LANES))],
        out_specs=[tok(d), tok(FNET_WIDTH), grp(N_KV_HEADS, 2 * LANES), grp(2 * N_KV_HEADS, LANES), vt_spec],
        out_shape=[jax.ShapeDtypeStruct((n_tok, d), F32),
                   jax.ShapeDtypeStruct((n_tok, FNET_WIDTH), BF16),
                   jax.ShapeDtypeStruct((N_KV_HEADS, n_tok, 2 * LANES), BF16),
                   jax.ShapeDtypeStruct((2 * N_KV_HEADS, n_tok, LANES), BF16),
                   jax.ShapeDtypeStruct((batch, N_KV_HEADS, HEAD_DIM, seq), BF16)],
        compiler_params=params,
        name="ffn1_inproj",
    )(x2d, _row(ffn1_norm), wg1, wu1, wd1, _row(mix_norm), w_in_bf, gain_qk, cos_t, sin_t, seg_ones)

    half = seq // 2
    dft_rows = half + BF16_SUBLANES
    cos_s, sin_s = _dft_cos_sin(seq, dft_rows, half)
    idx = np.arange(half)
    rev = jnp.asarray((idx[:, None] + idx[None, :]) == half, dtype=BF16)
    cos_c, sin_c = _dft_cos_sin(FNET_GROUP_DIM, FNET_GROUP_DIM, FNET_GROUP_DIM)
    ortho = 1.0 / math.sqrt(seq * FNET_GROUP_DIM)
    f_out = pl.pallas_call(
        _fourier_kernel,
        grid=(batch,),
        in_specs=[pl.BlockSpec((None, seq, FNET_WIDTH), lambda b: (b, 0, 0)),
                  _resident((dft_rows, half)), _resident((dft_rows, half)), _resident((half, half)),
                  _resident((FNET_GROUP_DIM, FNET_GROUP_DIM)), _resident((FNET_GROUP_DIM, FNET_GROUP_DIM)),
                  _resident((FNET_GROUPS, FNET_GROUP_DIM, FNET_GROUP_DIM)), _resident((1, FNET_WIDTH))],
        out_specs=pl.BlockSpec((None, seq, FNET_WIDTH), lambda b: (b, 0, 0)),
        out_shape=jax.ShapeDtypeStruct((batch, seq, FNET_WIDTH), BF16),
        scratch_shapes=[pltpu.VMEM((half, FNET_WIDTH), BF16), pltpu.VMEM((half, FNET_WIDTH), BF16)],
        compiler_params=params,
        name="fourier",
    )(uf.reshape(batch, seq, FNET_WIDTH), jnp.asarray(cos_s).astype(BF16), jnp.asarray(sin_s).astype(BF16), rev,
      cos_c * ortho, sin_c * ortho, fnet_w.astype(F32), fnet_b.reshape(1, FNET_WIDTH).astype(F32))

    a_out = pl.pallas_call(
        _attention_kernel,
        grid=(batch,),
        in_specs=[pl.BlockSpec((N_KV_HEADS, seq, 2 * LANES), lambda b: (0, b, 0)),
                  pl.BlockSpec((2 * N_KV_HEADS, seq, LANES), lambda b: (0, b, 0)),
                  pl.BlockSpec((None, N_KV_HEADS, HEAD_DIM, seq), lambda b: (b, 0, 0, 0))],
        out_specs=pl.BlockSpec((N_KV_HEADS, seq, 2 * LANES), lambda b: (0, b, 0)),
        out_shape=jax.ShapeDtypeStruct((N_KV_HEADS, n_tok, 2 * LANES), BF16),
        scratch_shapes=[pltpu.VMEM((seq, 2 * ATTN_Q_TILE), F32), pltpu.VMEM((seq, 2 * ATTN_Q_TILE), BF16),
                        pltpu.VMEM((2, 1, 2 * ATTN_Q_TILE), F32),
                        pltpu.VMEM((2, HEAD_DIM, 2 * ATTN_Q_TILE), F32)],
        compiler_params=params,
        name="attention",
    )(q, kp, vt)

    y = pl.pallas_call(
        _outproj_ffn2_kernel,
        grid=(n_tok // TOKEN_TILE,),
        in_specs=[tok(d), tok(FNET_WIDTH), grp(N_KV_HEADS, 2 * LANES),
                  _resident((FNET_WIDTH, d)), _resident((N_KV_HEADS, 2 * LANES, d)), _resident((1, d)),
                  _resident((d, D_FF_PAD)), _resident((d, D_FF_PAD)), _resident((D_FF_PAD, d)),
                  _resident((1, d))],
        out_specs=tok(d),
        out_shape=jax.ShapeDtypeStruct((n_tok, d), F32),
        compiler_params=params,
        name="outproj_ffn2",
    )(x1, f_out.reshape(n_tok, FNET_WIDTH), a_out,
      w_out_bf[:FNET_WIDTH], w_out_bf[FNET_WIDTH:].reshape(N_KV_HEADS, 2 * LANES, d), _row(ffn2_norm),
      wg2, wu2, wd2, _row(final_norm))
    return y.reshape(batch, seq, d)
```

```python
import math

import numpy as np
import jax
import jax.numpy as jnp
from jax import lax
from jax.experimental import pallas as pl
from jax.experimental.pallas import tpu as pltpu

D_MODEL = 1024
D_FF = 2752
FNET_WIDTH = 512
FNET_GROUPS = 4
FNET_GROUP_DIM = 128
HEAD_DIM = 64
N_Q_HEADS = 8
N_KV_HEADS = 2
GQA_GROUP = N_Q_HEADS // N_KV_HEADS
ATTN_WIDTH = N_Q_HEADS * HEAD_DIM
KV_WIDTH = N_KV_HEADS * HEAD_DIM
QK_WIDTH = ATTN_WIDTH + KV_WIDTH
KP_WIDTH = 2 * N_KV_HEADS * 128
IN_WIDTH = FNET_WIDTH + ATTN_WIDTH + 2 * KV_WIDTH
GRID_W = 64
AXIS_DIM = HEAD_DIM // 2
ROPE_THETA = 10000.0
EPS = 1e-6

LANES = 128
BF16_SUBLANES = 16
MXU_DIM = 256
VMEM_LIMIT_BYTES = 60000 * 1024

D_FF_PAD = ((D_FF + MXU_DIM - 1) // MXU_DIM) * MXU_DIM

TOKEN_TILE = 1024
SUB_TILE = 256
ATTN_Q_TILE = 256
WEIGHT_CAST_STEPS = 4

BF16 = jnp.bfloat16
F32 = jnp.float32


def _rms_norm(x, gain):
    return x * lax.rsqrt(jnp.mean(x * x, axis=-1, keepdims=True) + EPS) * gain


def _dot(a, b):
    return jnp.dot(a, b, preferred_element_type=F32)


def _swiglu_half_step(x, gain, wg_ref, wu_ref, wd_ref):
    xn = _rms_norm(x, gain).astype(BF16)
    g = _dot(xn, wg_ref[...])
    u = _dot(xn, wu_ref[...])
    a = (g * jax.nn.sigmoid(g) * u).astype(BF16)
    return x + 0.5 * _dot(a, wd_ref[...])


def _split_hi_lo(x):
    hi = x.astype(BF16)
    lo = (x - hi.astype(F32)).astype(BF16)
    return hi, lo


def _ffn1_inproj_kernel(x_ref, g1_ref, wg_ref, wu_ref, wd_ref, gm_ref, win_ref, gqk_ref,
                        cos_ref, sin_ref, seg_ref,
                        x1_ref, uf_ref, q_ref, kp_ref, vt_ref):
    seg = seg_ref[...]
    lane = lax.broadcasted_iota(jnp.int32, (1, LANES), 1)
    upper_half = (lane & (AXIS_DIM // 2)) != 0

    def gate_up(rows):
        xn = _rms_norm(x_ref[rows, :], g1_ref[...]).astype(BF16)
        g = _dot(xn, wg_ref[...])
        return (g * jax.nn.sigmoid(g) * _dot(xn, wu_ref[...])).astype(BF16)

    def down(rows, act):
        x1_ref[rows, :] = x_ref[rows, :] + 0.5 * _dot(act, wd_ref[...])

    def in_proj(rows):
        h = _rms_norm(x1_ref[rows, :], gm_ref[...]).astype(BF16)
        return _dot(h, win_ref[...])

    def heads(rows, u):
        uf_ref[rows, :] = u[:, :FNET_WIDTH].astype(BF16)
        vt = u[:, FNET_WIDTH + QK_WIDTH:].T.astype(BF16)
        for g in range(N_KV_HEADS):
            vt_ref[g, :, rows] = vt[g * HEAD_DIM:(g + 1) * HEAD_DIM]

        cos_t = cos_ref[rows, :]
        sin_t = sin_ref[rows, :]
        for c in range(QK_WIDTH // LANES):
            lo_col = FNET_WIDTH + c * LANES
            z = u[:, lo_col:lo_col + LANES]
            hi, lo = _split_hi_lo(z * z)
            mean_sq = (_dot(hi, seg) + _dot(lo, seg)) * (1.0 / HEAD_DIM)
            zn = z * lax.rsqrt(mean_sq + EPS) * gqk_ref[:, c * LANES:(c + 1) * LANES]
            partner = jnp.where(upper_half,
                                pltpu.roll(zn, AXIS_DIM // 2, axis=1),
                                pltpu.roll(zn, LANES - AXIS_DIM // 2, axis=1))
            out = zn * cos_t + partner * sin_t
            if c < ATTN_WIDTH // LANES:
                q_ref[c // 2, rows, (c % 2) * LANES:(c % 2 + 1) * LANES] = out.astype(BF16)
            else:
                swapped = pltpu.roll(out, HEAD_DIM, axis=1)
                first = lane < HEAD_DIM
                variants = (jnp.where(first, out, 0.0), jnp.where(first, 0.0, swapped),
                            jnp.where(first, swapped, 0.0), jnp.where(first, 0.0, out))
                for i, kv in enumerate(variants):
                    kp_ref[i, rows, :] = kv.astype(BF16)

    sub_tiles = [slice(r * SUB_TILE, (r + 1) * SUB_TILE) for r in range(TOKEN_TILE // SUB_TILE)]
    for slot in range(len(sub_tiles) + 1):
        cur = sub_tiles[slot] if slot < len(sub_tiles) else None
        prev = sub_tiles[slot - 1] if slot > 0 else None
        if cur is not None:
            act = gate_up(cur)
        if prev is not None:
            u_prev = in_proj(prev)
        if cur is not None:
            down(cur, act)
        if prev is not None:
            heads(prev, u_prev)


def _fourier_kernel(uf_ref, ch_ref, sh_ref, rev_ref, cc_ref, sc_ref, fw_ref, fb_ref, out_ref, e1_ref, o2_ref):
    n = uf_ref.shape[0]
    h = n // 2
    lo = uf_ref[:h, :]
    hi = uf_ref[h:, :]
    rev = rev_ref[...]
    mirrored = _dot(rev, hi)
    lo32 = lo.astype(F32)
    even = (lo32 + mirrored).astype(BF16)
    odd = (lo32 - mirrored).astype(BF16)

    cc_hi, cc_lo = _split_hi_lo(cc_ref[...])
    sc_hi, sc_lo = _split_hi_lo(sc_ref[...])
    mid_rows = []
    for g in range(FNET_GROUPS):
        cols = slice(g * FNET_GROUP_DIM, (g + 1) * FNET_GROUP_DIM)
        w_hi, w_lo = _split_hi_lo(fw_ref[g])
        wc = (_dot(cc_hi, w_hi) + _dot(cc_hi, w_lo) + _dot(cc_lo, w_hi)).astype(BF16)
        ws = (_dot(sc_hi, w_hi) + _dot(sc_hi, w_lo) + _dot(sc_lo, w_hi)).astype(BF16)
        e1_ref[:, cols] = _dot(even[:, cols], wc).astype(BF16)
        o2_ref[:, cols] = _dot(odd[:, cols], ws).astype(BF16)
        mid_rows.append(_dot(hi[:BF16_SUBLANES, cols], wc)[0:1])
    mid = jnp.concatenate(mid_rows, axis=-1)

    a = _dot(ch_ref[...], e1_ref[...])
    b = _dot(sh_ref[...], o2_ref[...])
    j = lax.broadcasted_iota(jnp.int32, (a.shape[0], 1), 0)
    a = a + jnp.where((j & 1) == 0, 1.0, -1.0) * mid + fb_ref[...]
    out_ref[:h, :] = (a[:h] - b[:h]).astype(BF16)
    z = (a + b).astype(BF16)
    out_ref[h:, :] = _dot(rev, z[:h]).astype(BF16)
    out_ref[h:h + 1, :] = z[h:h + 1]


def _attention_kernel(q_ref, kp_ref, vt_ref, o_ref, s_ref, p_ref, m_ref, att_ref):
    seq = kp_ref.shape[1]
    tq = ATTN_Q_TILE
    units_per_tile = 2 * N_KV_HEADS
    n_units = (seq // tq) * units_per_tile
    ones_rows = jnp.ones((BF16_SUBLANES, seq), BF16)

    def unit_index(u):
        t, r = u // units_per_tile, u % units_per_tile
        start = t * tq
        if not isinstance(start, int):
            start = pl.multiple_of(start, tq)
        return pl.ds(start, tq), r // 2, r % 2

    def qk(u):
        q_rows, g, half = unit_index(u)
        qblk = q_ref[g, q_rows, :]
        qg = jnp.concatenate([qblk[:, :LANES], qblk[:, LANES:]], axis=0)
        s = lax.dot_general(kp_ref[2 * g + half], qg, (((1,), (1,)), ((), ())), preferred_element_type=F32)
        s_ref[...] = s
        m_ref[half] = jnp.max(s, axis=0, keepdims=True)

    def ex(u):
        p_ref[...] = jnp.exp2(s_ref[...] - m_ref[u % 2]).astype(BF16)

    def pv(u):
        q_rows, g, half = unit_index(u)
        v1t = jnp.concatenate([vt_ref[g], ones_rows], axis=0)
        o = _dot(v1t, p_ref[...])
        att_ref[half] = o[:HEAD_DIM] * (1.0 / o[HEAD_DIM:HEAD_DIM + 1])
        for c in range(2):
            pair = jnp.concatenate([att_ref[h, :, c * tq:(c + 1) * tq] for h in range(2)], axis=0)
            o_ref[g, q_rows, c * LANES:(c + 1) * LANES] = pair.T.astype(BF16)

    def step(u, carry):
        pv(u - 1)
        ex(u)
        qk(u + 1)
        return carry

    att_ref[...] = jnp.zeros(att_ref.shape, F32)
    qk(0)
    ex(0)
    qk(1)
    lax.fori_loop(1, n_units - 1, step, 0)
    pv(n_units - 2)
    ex(n_units - 1)
    pv(n_units - 1)


def _outproj_ffn2_kernel(x1_ref, f_ref, a_ref, wof_ref, woa_ref, g2_ref, wg_ref, wu_ref, wd_ref,
                         gf_ref, y_ref):
    sub_tiles = [slice(r * SUB_TILE, (r + 1) * SUB_TILE) for r in range(TOKEN_TILE // SUB_TILE)]
    for rows in sub_tiles:
        x2 = x1_ref[rows, :] + _dot(f_ref[rows, :], wof_ref[...])
        for g in range(N_KV_HEADS):
            x2 = x2 + _dot(a_ref[g, rows, :], woa_ref[g])
        y_ref[rows, :] = x2
    for rows in sub_tiles:
        x3 = _swiglu_half_step(y_ref[rows, :], g2_ref[...], wg_ref, wu_ref, wd_ref)
        y_ref[rows, :] = _rms_norm(x3, gf_ref[...])


def _resident(shape):
    return pl.BlockSpec(shape, lambda *_: (0,) * len(shape), pipeline_mode=pl.Buffered(1))


def _rope_tables(seq):
    rows = seq // GRID_W
    row_idx = np.repeat(np.arange(rows, dtype=np.float64), GRID_W)
    col_idx = np.tile(np.arange(GRID_W, dtype=np.float64), rows)
    inv_freq = ROPE_THETA ** (-np.arange(0, AXIS_DIM, 2, dtype=np.float64) / AXIS_DIM)
    ang_r = row_idx[:, None] * inv_freq[None, :]
    ang_c = col_idx[:, None] * inv_freq[None, :]
    cos_h = np.concatenate([np.cos(ang_r)] * 2 + [np.cos(ang_c)] * 2, axis=-1)
    sin_h = np.concatenate([-np.sin(ang_r), np.sin(ang_r), -np.sin(ang_c), np.sin(ang_c)], axis=-1)
    reps = LANES // HEAD_DIM
    return (np.tile(cos_h, (1, reps)).astype(np.float32), np.tile(sin_h, (1, reps)).astype(np.float32))


def _dft_cos_sin(n, rows, cols):
    j = np.arange(rows, dtype=np.int64)
    k = np.arange(cols, dtype=np.int64)
    ang = ((j[:, None] * k[None, :]) % n).astype(np.float64) * (2.0 * math.pi / n)
    return np.cos(ang).astype(np.float32), np.sin(ang).astype(np.float32)


def _cast_pad_cols_kernel(*refs):
    n = len(refs) // 2
    aligned = (D_FF // LANES) * LANES
    for wt_ref, o_ref in zip(refs[:n], refs[n:]):
        o_ref[:, :aligned] = wt_ref[:aligned, :].T.astype(BF16)
        tail = jnp.concatenate([wt_ref[aligned:, :], jnp.zeros((D_FF_PAD - D_FF, wt_ref.shape[1]), F32)], axis=0)
        o_ref[:, aligned:] = tail.T.astype(BF16)


def _cast_pad_rows_kernel(*refs):
    n = len(refs) // 2
    for w_ref, o_ref in zip(refs[:n], refs[n:]):
        o_ref[:D_FF, :] = w_ref[...].astype(BF16)
        o_ref[D_FF:, :] = jnp.zeros((D_FF_PAD - D_FF, o_ref.shape[1]), BF16)


def _cast_kernel(*refs):
    n = len(refs) // 2
    for w_ref, o_ref in zip(refs[:n], refs[n:]):
        o_ref[...] = w_ref[...].astype(BF16)


def _cast_weights(kernel_fn, weights, in_block, in_map, out_shape, out_block, out_map, steps, name):
    return pl.pallas_call(
        kernel_fn,
        grid=(steps,),
        in_specs=[pl.BlockSpec(in_block, in_map)] * len(weights),
        out_specs=[pl.BlockSpec(out_block, out_map)] * len(weights),
        out_shape=[jax.ShapeDtypeStruct(out_shape, BF16)] * len(weights),
        compiler_params=pltpu.CompilerParams(dimension_semantics=("arbitrary",),
                                             vmem_limit_bytes=VMEM_LIMIT_BYTES),
        name=name,
    )(*weights)


def _row(v):
    return v.reshape(1, -1).astype(F32)


def kernel(x, ffn1_norm, ffn1_w_gate, ffn1_w_up, ffn1_w_down, mix_norm, w_in, fnet_w, fnet_b, q_norm,
           k_norm, w_out, ffn2_norm, ffn2_w_gate, ffn2_w_up, ffn2_w_down, final_norm):
    batch, seq, d = x.shape
    assert d == D_MODEL and seq % TOKEN_TILE == 0 and seq % ATTN_Q_TILE == 0 and seq % GRID_W == 0
    n_tok = batch * seq
    tiles_per_seq = seq // TOKEN_TILE
    x2d = x.reshape(n_tok, d)

    params = pltpu.CompilerParams(dimension_semantics=("arbitrary",), vmem_limit_bytes=VMEM_LIMIT_BYTES)
    tok = lambda width: pl.BlockSpec((TOKEN_TILE, width), lambda i: (i, 0))

    blk = d // WEIGHT_CAST_STEPS
    by_rows, by_cols = (lambda i: (i, 0)), (lambda i: (0, i))
    wg1, wu1, wg2, wu2 = _cast_weights(
        _cast_pad_cols_kernel, [w.T for w in (ffn1_w_gate, ffn1_w_up, ffn2_w_gate, ffn2_w_up)],
        (D_FF, blk), by_cols, (d, D_FF_PAD), (blk, D_FF_PAD), by_rows, WEIGHT_CAST_STEPS, "cast_gate_up")
    wd1, wd2 = _cast_weights(
        _cast_pad_rows_kernel, [ffn1_w_down, ffn2_w_down],
        (D_FF, blk), by_cols, (D_FF_PAD, d), (D_FF_PAD, blk), by_cols, WEIGHT_CAST_STEPS, "cast_down")
    w_in_bf, = _cast_weights(_cast_kernel, [w_in], (blk, IN_WIDTH), by_rows, (d, IN_WIDTH), (blk, IN_WIDTH),
                             by_rows, WEIGHT_CAST_STEPS, "cast_w_in")
    w_out_bf, = _cast_weights(_cast_kernel, [w_out], (blk, d), by_rows, (d, d), (blk, d), by_rows,
                              WEIGHT_CAST_STEPS, "cast_w_out")

    cos_t, sin_t = _rope_tables(seq)
    scale = math.log2(math.e) / math.sqrt(HEAD_DIM)
    gain_qk = jnp.concatenate([jnp.tile(q_norm.astype(F32) * scale, N_Q_HEADS),
                               jnp.tile(k_norm.astype(F32), N_KV_HEADS)]).reshape(1, QK_WIDTH)
    head_of_lane = jnp.arange(LANES) // HEAD_DIM
    seg_ones = (head_of_lane[:, None] == head_of_lane[None, :]).astype(BF16)
    rope_spec = pl.BlockSpec((TOKEN_TILE, LANES), lambda i: (i % tiles_per_seq, 0))

    grp = lambda n, width: pl.BlockSpec((n, TOKEN_TILE, width), lambda i: (0, i, 0))
    vt_spec = pl.BlockSpec((None, N_KV_HEADS, HEAD_DIM, TOKEN_TILE),
                           lambda i: (i // tiles_per_seq, 0, 0, i % tiles_per_seq))
    x1, uf, q, kp, vt = pl.pallas_call(
        _ffn1_inproj_kernel,
        grid=(n_tok // TOKEN_TILE,),
        in_specs=[tok(d), _resident((1, d)),
                  _resident((d, D_FF_PAD)), _resident((d, D_FF_PAD)), _resident((D_FF_PAD, d)),
                  _resident((1, d)), _resident((d, IN_WIDTH)), _resident((1, QK_WIDTH)),
                  rope_spec, rope_spec, _resident((LANES, LANES))],
        out_specs=[tok(d), tok(FNET_WIDTH), grp(N_KV_HEADS, 2 * LANES), grp(2 * N_KV_HEADS, LANES), vt_spec],
        out_shape=[jax.ShapeDtypeStruct((n_tok, d), F32),
                   jax.ShapeDtypeStruct((n_tok, FNET_WIDTH), BF16),
                   jax.ShapeDtypeStruct((N_KV_HEADS, n_tok, 2 * LANES), BF16),
                   jax.ShapeDtypeStruct((2 * N_KV_HEADS, n_tok, LANES), BF16),
                   jax.ShapeDtypeStruct((batch, N_KV_HEADS, HEAD_DIM, seq), BF16)],
        compiler_params=params,
        name="ffn1_inproj",
    )(x2d, _row(ffn1_norm), wg1, wu1, wd1, _row(mix_norm), w_in_bf, gain_qk, cos_t, sin_t, seg_ones)

    half = seq // 2
    dft_rows = half + BF16_SUBLANES
    cos_s, sin_s = _dft_cos_sin(seq, dft_rows, half)
    idx = np.arange(half)
    rev = jnp.asarray((idx[:, None] + idx[None, :]) == half, dtype=BF16)
    cos_c, sin_c = _dft_cos_sin(FNET_GROUP_DIM, FNET_GROUP_DIM, FNET_GROUP_DIM)
    ortho = 1.0 / math.sqrt(seq * FNET_GROUP_DIM)
    f_out = pl.pallas_call(
        _fourier_kernel,
        grid=(batch,),
        in_specs=[pl.BlockSpec((None, seq, FNET_WIDTH), lambda b: (b, 0, 0)),
                  _resident((dft_rows, half)), _resident((dft_rows, half)), _resident((half, half)),
                  _resident((FNET_GROUP_DIM, FNET_GROUP_DIM)), _resident((FNET_GROUP_DIM, FNET_GROUP_DIM)),
                  _resident((FNET_GROUPS, FNET_GROUP_DIM, FNET_GROUP_DIM)), _resident((1, FNET_WIDTH))],
        out_specs=pl.BlockSpec((None, seq, FNET_WIDTH), lambda b: (b, 0, 0)),
        out_shape=jax.ShapeDtypeStruct((batch, seq, FNET_WIDTH), BF16),
        scratch_shapes=[pltpu.VMEM((half, FNET_WIDTH), BF16), pltpu.VMEM((half, FNET_WIDTH), BF16)],
        compiler_params=params,
        name="fourier",
    )(uf.reshape(batch, seq, FNET_WIDTH), jnp.asarray(cos_s).astype(BF16), jnp.asarray(sin_s).astype(BF16), rev,
      cos_c * ortho, sin_c * ortho, fnet_w.astype(F32), fnet_b.reshape(1, FNET_WIDTH).astype(F32))

    a_out = pl.pallas_call(
        _attention_kernel,
        grid=(batch,),
        in_specs=[pl.BlockSpec((N_KV_HEADS, seq, 2 * LANES), lambda b: (0, b, 0)),
                  pl.BlockSpec((2 * N_KV_HEADS, seq, LANES), lambda b: (0, b, 0)),
                  pl.BlockSpec((None, N_KV_HEADS, HEAD_DIM, seq), lambda b: (b, 0, 0, 0))],
        out_specs=pl.BlockSpec((N_KV_HEADS, seq, 2 * LANES), lambda b: (0, b, 0)),
        out_shape=jax.ShapeDtypeStruct((N_KV_HEADS, n_tok, 2 * LANES), BF16),
        scratch_shapes=[pltpu.VMEM((seq, 2 * ATTN_Q_TILE), F32), pltpu.VMEM((seq, 2 * ATTN_Q_TILE), BF16),
                        pltpu.VMEM((2, 1, 2 * ATTN_Q_TILE), F32),
                        pltpu.VMEM((2, HEAD_DIM, 2 * ATTN_Q_TILE), F32)],
        compiler_params=params,
        name="attention",
    )(q, kp, vt)

    y = pl.pallas_call(
        _outproj_ffn2_kernel,
        grid=(n_tok // TOKEN_TILE,),
        in_specs=[tok(d), tok(FNET_WIDTH), grp(N_KV_HEADS, 2 * LANES),
                  _resident((FNET_WIDTH, d)), _resident((N_KV_HEADS, 2 * LANES, d)), _resident((1, d)),
                  _resident((d, D_FF_PAD)), _resident((d, D_FF_PAD)), _resident((D_FF_PAD, d)),
                  _resident((1, d))],
        out_specs=tok(d),
        out_shape=jax.ShapeDtypeStruct((n_tok, d), F32),
        compiler_params=params,
        name="outproj_ffn2",
    )(x1, f_out.reshape(n_tok, FNET_WIDTH), a_out,
      w_out_bf[:FNET_WIDTH], w_out_bf[FNET_WIDTH:].reshape(N_KV_HEADS, 2 * LANES, d), _row(ffn2_norm),
      wg2, wu2, wd2, _row(final_norm))
    return y.reshape(batch, seq, d)
```

```python
import math

import numpy as np
import jax
import jax.numpy as jnp
from jax import lax
from jax.experimental import pallas as pl
from jax.experimental.pallas import tpu as pltpu

D_MODEL = 1024
D_FF = 2752
FNET_WIDTH = 512
FNET_GROUPS = 4
FNET_GROUP_DIM = 128
HEAD_DIM = 64
N_Q_HEADS = 8
N_KV_HEADS = 2
GQA_GROUP = N_Q_HEADS // N_KV_HEADS
ATTN_WIDTH = N_Q_HEADS * HEAD_DIM
KV_WIDTH = N_KV_HEADS * HEAD_DIM
QK_WIDTH = ATTN_WIDTH + KV_WIDTH
KP_WIDTH = 2 * N_KV_HEADS * 128
IN_WIDTH = FNET_WIDTH + ATTN_WIDTH + 2 * KV_WIDTH
GRID_W = 64
AXIS_DIM = HEAD_DIM // 2
ROPE_THETA = 10000.0
EPS = 1e-6

LANES = 128
BF16_SUBLANES = 16
MXU_DIM = 256
VMEM_LIMIT_BYTES = 60000 * 1024

D_FF_PAD = ((D_FF + MXU_DIM - 1) // MXU_DIM) * MXU_DIM

TOKEN_TILE = 1024
SUB_TILE = 256
ATTN_Q_TILE = 256
WEIGHT_CAST_STEPS = 4
FOURIER_BATCH = 2

BF16 = jnp.bfloat16
F32 = jnp.float32


def _rms_norm(x, gain):
    return x * lax.rsqrt(jnp.mean(x * x, axis=-1, keepdims=True) + EPS) * gain


def _dot(a, b):
    return jnp.dot(a, b, preferred_element_type=F32)


def _swiglu_half_step(x, gain, wg_ref, wu_ref, wd_ref):
    xn = _rms_norm(x, gain).astype(BF16)
    g = _dot(xn, wg_ref[...])
    u = _dot(xn, wu_ref[...])
    a = (g * jax.nn.sigmoid(g) * u).astype(BF16)
    return x + 0.5 * _dot(a, wd_ref[...])


def _split_hi_lo(x):
    hi = x.astype(BF16)
    lo = (x - hi.astype(F32)).astype(BF16)
    return hi, lo


def _ffn1_inproj_kernel(x_ref, g1_ref, wg_ref, wu_ref, wd_ref, gm_ref, win_ref, gqk_ref,
                        cos_ref, sin_ref, seg_ref,
                        x1_ref, uf_ref, q_ref, kp_ref, vt_ref):
    seg = seg_ref[...]
    lane = lax.broadcasted_iota(jnp.int32, (1, LANES), 1)
    upper_half = (lane & (AXIS_DIM // 2)) != 0

    def gate_up(rows):
        xn = _rms_norm(x_ref[rows, :], g1_ref[...]).astype(BF16)
        g = _dot(xn, wg_ref[...])
        return (g * jax.nn.sigmoid(g) * _dot(xn, wu_ref[...])).astype(BF16)

    def down(rows, act):
        x1_ref[rows, :] = x_ref[rows, :] + 0.5 * _dot(act, wd_ref[...])

    def in_proj(rows):
        h = _rms_norm(x1_ref[rows, :], gm_ref[...]).astype(BF16)
        return _dot(h, win_ref[...])

    def heads(rows, u):
        uf_ref[rows, :] = u[:, :FNET_WIDTH].astype(BF16)
        vt = u[:, FNET_WIDTH + QK_WIDTH:].T.astype(BF16)
        for g in range(N_KV_HEADS):
            vt_ref[g, :, rows] = vt[g * HEAD_DIM:(g + 1) * HEAD_DIM]

        cos_t = cos_ref[rows, :]
        sin_t = sin_ref[rows, :]
        for c in range(QK_WIDTH // LANES):
            lo_col = FNET_WIDTH + c * LANES
            z = u[:, lo_col:lo_col + LANES]
            hi, lo = _split_hi_lo(z * z)
            mean_sq = (_dot(hi, seg) + _dot(lo, seg)) * (1.0 / HEAD_DIM)
            zn = z * lax.rsqrt(mean_sq + EPS) * gqk_ref[:, c * LANES:(c + 1) * LANES]
            partner = jnp.where(upper_half,
                                pltpu.roll(zn, AXIS_DIM // 2, axis=1),
                                pltpu.roll(zn, LANES - AXIS_DIM // 2, axis=1))
            out = zn * cos_t + partner * sin_t
            if c < ATTN_WIDTH // LANES:
                q_ref[c // 2, rows, (c % 2) * LANES:(c % 2 + 1) * LANES] = out.astype(BF16)
            else:
                swapped = pltpu.roll(out, HEAD_DIM, axis=1)
                first = lane < HEAD_DIM
                variants = (jnp.where(first, out, 0.0), jnp.where(first, 0.0, swapped),
                            jnp.where(first, swapped, 0.0), jnp.where(first, 0.0, out))
                for i, kv in enumerate(variants):
                    kp_ref[i, rows, :] = kv.astype(BF16)

    sub_tiles = [slice(r * SUB_TILE, (r + 1) * SUB_TILE) for r in range(TOKEN_TILE // SUB_TILE)]
    for slot in range(len(sub_tiles) + 1):
        cur = sub_tiles[slot] if slot < len(sub_tiles) else None
        prev = sub_tiles[slot - 1] if slot > 0 else None
        if cur is not None:
            act = gate_up(cur)
        if prev is not None:
            u_prev = in_proj(prev)
        if cur is not None:
            down(cur, act)
        if prev is not None:
            heads(prev, u_prev)


def _fourier_kernel(uf_ref, ch_ref, sh_ref, rev_ref, cc_ref, sc_ref, fw_ref, fb_ref, out_ref, e1_ref, o2_ref):
    n_batch, n = uf_ref.shape[0], uf_ref.shape[1]
    h = n // 2
    rev = rev_ref[...]

    cc_hi, cc_lo = _split_hi_lo(cc_ref[...])
    sc_hi, sc_lo = _split_hi_lo(sc_ref[...])
    folded = []
    for g in range(FNET_GROUPS):
        w_hi, w_lo = _split_hi_lo(fw_ref[g])
        folded.append(((_dot(cc_hi, w_hi) + _dot(cc_hi, w_lo) + _dot(cc_lo, w_hi)).astype(BF16),
                       (_dot(sc_hi, w_hi) + _dot(sc_hi, w_lo) + _dot(sc_lo, w_hi)).astype(BF16)))

    def mirror(b):
        mirrored = _dot(rev, uf_ref[b, h:, :])
        lo32 = uf_ref[b, :h, :].astype(F32)
        return (lo32 + mirrored).astype(BF16), (lo32 - mirrored).astype(BF16)

    def channel_mix(b, even, odd):
        mid_rows = []
        for g, (wc, ws) in enumerate(folded):
            cols = slice(g * FNET_GROUP_DIM, (g + 1) * FNET_GROUP_DIM)
            e1_ref[b, :, cols] = _dot(even[:, cols], wc).astype(BF16)
            o2_ref[b, :, cols] = _dot(odd[:, cols], ws).astype(BF16)
            mid_rows.append(_dot(uf_ref[b, h:h + BF16_SUBLANES, cols], wc)[0:1])
        return jnp.concatenate(mid_rows, axis=-1)

    def position_dft(b, mid):
        a = _dot(ch_ref[...], e1_ref[b])
        s = _dot(sh_ref[...], o2_ref[b])
        j = lax.broadcasted_iota(jnp.int32, (a.shape[0], 1), 0)
        a = a + jnp.where((j & 1) == 0, 1.0, -1.0) * mid + fb_ref[...]
        out_ref[b, :h, :] = (a[:h] - s[:h]).astype(BF16)
        return (a + s).astype(BF16)

    def mirror_out(b, z):
        out_ref[b, h:, :] = _dot(rev, z[:h]).astype(BF16)
        out_ref[b, h:h + 1, :] = z[h:h + 1]

    halves = [mirror(b) for b in range(n_batch)]
    mids = [channel_mix(b, *halves[b]) for b in range(n_batch)]
    zs = [position_dft(b, mids[b]) for b in range(n_batch)]
    for b in range(n_batch):
        mirror_out(b, zs[b])


def _attention_kernel(q_ref, kp_ref, vt_ref, o_ref, s0_ref, s1_ref, p0_ref, p1_ref, m_ref, att_ref):
    seq = kp_ref.shape[1]
    tq = ATTN_Q_TILE
    n_pairs = (seq // tq) * N_KV_HEADS
    s_refs, p_refs = (s0_ref, s1_ref), (p0_ref, p1_ref)
    ones_rows = jnp.ones((BF16_SUBLANES, seq), BF16)
    halves = range(2)

    def pair_index(j):
        t, g = j // N_KV_HEADS, j % N_KV_HEADS
        start = t * tq
        if not isinstance(start, int):
            start = pl.multiple_of(start, tq)
        return pl.ds(start, tq), g

    def qk(j, half):
        q_rows, g = pair_index(j)
        qblk = q_ref[g, q_rows, :]
        qg = jnp.concatenate([qblk[:, :LANES], qblk[:, LANES:]], axis=0)
        s = lax.dot_general(kp_ref[2 * g + half], qg, (((1,), (1,)), ((), ())), preferred_element_type=F32)
        s_refs[half][...] = s
        m_ref[half, j % 2] = jnp.max(s, axis=0, keepdims=True)

    def ex(j, half):
        p_refs[half][...] = jnp.exp2(s_refs[half][...] - m_ref[half, j % 2]).astype(BF16)

    def pv(j, half):
        _, g = pair_index(j)
        v1t = jnp.concatenate([vt_ref[g], ones_rows], axis=0)
        o = _dot(v1t, p_refs[half][...])
        att_ref[half] = o[:HEAD_DIM] * (1.0 / o[HEAD_DIM:HEAD_DIM + 1])

    def emit(j):
        q_rows, g = pair_index(j)
        for c in range(2):
            pair = jnp.concatenate([att_ref[h, :, c * tq:(c + 1) * tq] for h in halves], axis=0)
            o_ref[g, q_rows, c * LANES:(c + 1) * LANES] = pair.T.astype(BF16)

    def step(j, first=False, last=False):
        ex(j, 0)
        if not last:
            qk(j + 1, 0)
        if not first:
            pv(j - 1, 1)
            emit(j - 1)
        ex(j, 1)
        if not last:
            qk(j + 1, 1)
        pv(j, 0)

    def loop_body(j, carry):
        step(j)
        return carry

    for half in halves:
        qk(0, half)
    step(0, first=True)
    lax.fori_loop(1, n_pairs - 1, loop_body, 0)
    step(n_pairs - 1, last=True)
    pv(n_pairs - 1, 1)
    emit(n_pairs - 1)


def _outproj_ffn2_kernel(x1_ref, f_ref, a_ref, wof_ref, woa_ref, g2_ref, wg_ref, wu_ref, wd_ref,
                         gf_ref, y_ref):
    sub_tiles = [slice(r * SUB_TILE, (r + 1) * SUB_TILE) for r in range(TOKEN_TILE // SUB_TILE)]
    for rows in sub_tiles:
        x2 = x1_ref[rows, :] + _dot(f_ref[rows, :], wof_ref[...])
        for g in range(N_KV_HEADS):
            x2 = x2 + _dot(a_ref[g, rows, :], woa_ref[g])
        y_ref[rows, :] = x2
    for rows in sub_tiles:
        x3 = _swiglu_half_step(y_ref[rows, :], g2_ref[...], wg_ref, wu_ref, wd_ref)
        y_ref[rows, :] = _rms_norm(x3, gf_ref[...])


def _resident(shape):
    return pl.BlockSpec(shape, lambda *_: (0,) * len(shape), pipeline_mode=pl.Buffered(1))


def _rope_tables(seq):
    rows = seq // GRID_W
    row_idx = np.repeat(np.arange(rows, dtype=np.float64), GRID_W)
    col_idx = np.tile(np.arange(GRID_W, dtype=np.float64), rows)
    inv_freq = ROPE_THETA ** (-np.arange(0, AXIS_DIM, 2, dtype=np.float64) / AXIS_DIM)
    ang_r = row_idx[:, None] * inv_freq[None, :]
    ang_c = col_idx[:, None] * inv_freq[None, :]
    cos_h = np.concatenate([np.cos(ang_r)] * 2 + [np.cos(ang_c)] * 2, axis=-1)
    sin_h = np.concatenate([-np.sin(ang_r), np.sin(ang_r), -np.sin(ang_c), np.sin(ang_c)], axis=-1)
    reps = LANES // HEAD_DIM
    return (np.tile(cos_h, (1, reps)).astype(np.float32), np.tile(sin_h, (1, reps)).astype(np.float32))


def _dft_cos_sin(n, rows, cols):
    j = np.arange(rows, dtype=np.int64)
    k = np.arange(cols, dtype=np.int64)
    ang = ((j[:, None] * k[None, :]) % n).astype(np.float64) * (2.0 * math.pi / n)
    return np.cos(ang).astype(np.float32), np.sin(ang).astype(np.float32)


def _cast_pad_cols_kernel(*refs):
    n = len(refs) // 2
    aligned = (D_FF // LANES) * LANES
    for wt_ref, o_ref in zip(refs[:n], refs[n:]):
        o_ref[:, :aligned] = wt_ref[:aligned, :].T.astype(BF16)
        tail = jnp.concatenate([wt_ref[aligned:, :], jnp.zeros((D_FF_PAD - D_FF, wt_ref.shape[1]), F32)], axis=0)
        o_ref[:, aligned:] = tail.T.astype(BF16)


def _cast_pad_rows_kernel(*refs):
    n = len(refs) // 2
    for w_ref, o_ref in zip(refs[:n], refs[n:]):
        o_ref[:D_FF, :] = w_ref[...].astype(BF16)
        o_ref[D_FF:, :] = jnp.zeros((D_FF_PAD - D_FF, o_ref.shape[1]), BF16)


def _cast_kernel(*refs):
    n = len(refs) // 2
    for w_ref, o_ref in zip(refs[:n], refs[n:]):
        o_ref[...] = w_ref[...].astype(BF16)


def _cast_weights(kernel_fn, weights, in_block, in_map, out_shape, out_block, out_map, steps, name):
    return pl.pallas_call(
        kernel_fn,
        grid=(steps,),
        in_specs=[pl.BlockSpec(in_block, in_map)] * len(weights),
        out_specs=[pl.BlockSpec(out_block, out_map)] * len(weights),
        out_shape=[jax.ShapeDtypeStruct(out_shape, BF16)] * len(weights),
        compiler_params=pltpu.CompilerParams(dimension_semantics=("arbitrary",),
                                             vmem_limit_bytes=VMEM_LIMIT_BYTES),
        name=name,
    )(*weights)


def _row(v):
    return v.reshape(1, -1).astype(F32)


def kernel(x, ffn1_norm, ffn1_w_gate, ffn1_w_up, ffn1_w_down, mix_norm, w_in, fnet_w, fnet_b, q_norm,
           k_norm, w_out, ffn2_norm, ffn2_w_gate, ffn2_w_up, ffn2_w_down, final_norm):
    batch, seq, d = x.shape
    assert d == D_MODEL and seq % TOKEN_TILE == 0 and seq % ATTN_Q_TILE == 0 and seq % GRID_W == 0
    n_tok = batch * seq
    tiles_per_seq = seq // TOKEN_TILE
    x2d = x.reshape(n_tok, d)

    params = pltpu.CompilerParams(dimension_semantics=("arbitrary",), vmem_limit_bytes=VMEM_LIMIT_BYTES)
    tok = lambda width: pl.BlockSpec((TOKEN_TILE, width), lambda i: (i, 0))

    blk = d // WEIGHT_CAST_STEPS
    by_rows, by_cols = (lambda i: (i, 0)), (lambda i: (0, i))
    wg1, wu1, wg2, wu2 = _cast_weights(
        _cast_pad_cols_kernel, [w.T for w in (ffn1_w_gate, ffn1_w_up, ffn2_w_gate, ffn2_w_up)],
        (D_FF, blk), by_cols, (d, D_FF_PAD), (blk, D_FF_PAD), by_rows, WEIGHT_CAST_STEPS, "cast_gate_up")
    wd1, wd2 = _cast_weights(
        _cast_pad_rows_kernel, [ffn1_w_down, ffn2_w_down],
        (D_FF, blk), by_cols, (D_FF_PAD, d), (D_FF_PAD, blk), by_cols, WEIGHT_CAST_STEPS, "cast_down")
    w_in_bf, = _cast_weights(_cast_kernel, [w_in], (blk, IN_WIDTH), by_rows, (d, IN_WIDTH), (blk, IN_WIDTH),
                             by_rows, WEIGHT_CAST_STEPS, "cast_w_in")
    w_out_bf, = _cast_weights(_cast_kernel, [w_out], (blk, d), by_rows, (d, d), (blk, d), by_rows,
                              WEIGHT_CAST_STEPS, "cast_w_out")

    cos_t, sin_t = _rope_tables(seq)
    scale = math.log2(math.e) / math.sqrt(HEAD_DIM)
    gain_qk = jnp.concatenate([jnp.tile(q_norm.astype(F32) * scale, N_Q_HEADS),
                               jnp.tile(k_norm.astype(F32), N_KV_HEADS)]).reshape(1, QK_WIDTH)
    head_of_lane = np.arange(LANES) // HEAD_DIM
    seg_ones = jnp.asarray(head_of_lane[:, None] == head_of_lane[None, :], dtype=BF16)
    rope_spec = pl.BlockSpec((TOKEN_TILE, LANES), lambda i: (i % tiles_per_seq, 0))

    grp = lambda n, width: pl.BlockSpec((n, TOKEN_TILE, width), lambda i: (0, i, 0))
    vt_spec = pl.BlockSpec((None, N_KV_HEADS, HEAD_DIM, TOKEN_TILE),
                           lambda i: (i // tiles_per_seq, 0, 0, i % tiles_per_seq))
    x1, uf, q, kp, vt = pl.pallas_call(
        _ffn1_inproj_kernel,
        grid=(n_tok // TOKEN_TILE,),
        in_specs=[tok(d), _resident((1, d)),
                  _resident((d, D_FF_PAD)), _resident((d, D_FF_PAD)), _resident((D_FF_PAD, d)),
                  _resident((1, d)), _resident((d, IN_WIDTH)), _resident((1, QK_WIDTH)),
                  rope_spec, rope_spec, _resident((LANES, LANES))],
        out_specs=[tok(d), tok(FNET_WIDTH), grp(N_KV_HEADS, 2 * LANES), grp(2 * N_KV_HEADS, LANES), vt_spec],
        out_shape=[jax.ShapeDtypeStruct((n_tok, d), F32),
                   jax.ShapeDtypeStruct((n_tok, FNET_WIDTH), BF16),
                   jax.ShapeDtypeStruct((N_KV_HEADS, n_tok, 2 * LANES), BF16),
                   jax.ShapeDtypeStruct((2 * N_KV_HEADS, n_tok, LANES), BF16),
                   jax.ShapeDtypeStruct((batch, N_KV_HEADS, HEAD_DIM, seq), BF16)],
        compiler_params=params,
        name="ffn1_inproj",
    )(x2d, _row(ffn1_norm), wg1, wu1, wd1, _row(mix_norm), w_in_bf, gain_qk, cos_t, sin_t, seg_ones)

    half = seq // 2
    dft_rows = half + BF16_SUBLANES
    cos_s, sin_s = _dft_cos_sin(seq, dft_rows, half)
    idx = np.arange(half)
    rev = jnp.asarray((idx[:, None] + idx[None, :]) == half, dtype=BF16)
    cos_c, sin_c = _dft_cos_sin(FNET_GROUP_DIM, FNET_GROUP_DIM, FNET_GROUP_DIM)
    ortho = 1.0 / math.sqrt(seq * FNET_GROUP_DIM)
    f_out = pl.pallas_call(
        _fourier_kernel,
        grid=(batch // FOURIER_BATCH,),
        in_specs=[pl.BlockSpec((FOURIER_BATCH, seq, FNET_WIDTH), lambda b: (b, 0, 0)),
                  _resident((dft_rows, half)), _resident((dft_rows, half)), _resident((half, half)),
                  _resident((FNET_GROUP_DIM, FNET_GROUP_DIM)), _resident((FNET_GROUP_DIM, FNET_GROUP_DIM)),
                  _resident((FNET_GROUPS, FNET_GROUP_DIM, FNET_GROUP_DIM)), _resident((1, FNET_WIDTH))],
        out_specs=pl.BlockSpec((FOURIER_BATCH, seq, FNET_WIDTH), lambda b: (b, 0, 0)),
        out_shape=jax.ShapeDtypeStruct((batch, seq, FNET_WIDTH), BF16),
        scratch_shapes=[pltpu.VMEM((FOURIER_BATCH, half, FNET_WIDTH), BF16),
                        pltpu.VMEM((FOURIER_BATCH, half, FNET_WIDTH), BF16)],
        compiler_params=params,
        name="fourier",
    )(uf.reshape(batch, seq, FNET_WIDTH), jnp.asarray(cos_s).astype(BF16), jnp.asarray(sin_s).astype(BF16), rev,
      cos_c * ortho, sin_c * ortho, fnet_w.astype(F32), fnet_b.reshape(1, FNET_WIDTH).astype(F32))

    a_out = pl.pallas_call(
        _attention_kernel,
        grid=(batch,),
        in_specs=[pl.BlockSpec((N_KV_HEADS, seq, 2 * LANES), lambda b: (0, b, 0)),
                  pl.BlockSpec((2 * N_KV_HEADS, seq, LANES), lambda b: (0, b, 0)),
                  pl.BlockSpec((None, N_KV_HEADS, HEAD_DIM, seq), lambda b: (b, 0, 0, 0))],
        out_specs=pl.BlockSpec((N_KV_HEADS, seq, 2 * LANES), lambda b: (0, b, 0)),
        out_shape=jax.ShapeDtypeStruct((N_KV_HEADS, n_tok, 2 * LANES), BF16),
        scratch_shapes=[pltpu.VMEM((seq, 2 * ATTN_Q_TILE), F32), pltpu.VMEM((seq, 2 * ATTN_Q_TILE), F32),
                        pltpu.VMEM((seq, 2 * ATTN_Q_TILE), BF16), pltpu.VMEM((seq, 2 * ATTN_Q_TILE), BF16),
                        pltpu.VMEM((2, 2, 1, 2 * ATTN_Q_TILE), F32),
                        pltpu.VMEM((2, HEAD_DIM, 2 * ATTN_Q_TILE), F32)],
        compiler_params=params,
        name="attention",
    )(q, kp, vt)

    y = pl.pallas_call(
        _outproj_ffn2_kernel,
        grid=(n_tok // TOKEN_TILE,),
        in_specs=[tok(d), tok(FNET_WIDTH), grp(N_KV_HEADS, 2 * LANES),
                  _resident((FNET_WIDTH, d)), _resident((N_KV_HEADS, 2 * LANES, d)), _resident((1, d)),
                  _resident((d, D_FF_PAD)), _resident((d, D_FF_PAD)), _resident((D_FF_PAD, d)),
                  _resident((1, d))],
        out_specs=tok(d),
        out_shape=jax.ShapeDtypeStruct((n_tok, d), F32),
        compiler_params=params,
        name="outproj_ffn2",
    )(x1, f_out.reshape(n_tok, FNET_WIDTH), a_out,
      w_out_bf[:FNET_WIDTH], w_out_bf[FNET_WIDTH:].reshape(N_KV_HEADS, 2 * LANES, d), _row(ffn2_norm),
      wg2, wu2, wd2, _row(final_norm))
    return y.reshape(batch, seq, d)
```

```python
import math

import numpy as np
import jax
import jax.numpy as jnp
from jax import lax
from jax.experimental import pallas as pl
from jax.experimental.pallas import tpu as pltpu

D_MODEL = 1024
D_FF = 2752
FNET_WIDTH = 512
FNET_GROUPS = 4
FNET_GROUP_DIM = 128
HEAD_DIM = 64
N_Q_HEADS = 8
N_KV_HEADS = 2
ATTN_WIDTH = N_Q_HEADS * HEAD_DIM
KV_WIDTH = N_KV_HEADS * HEAD_DIM
QK_WIDTH = ATTN_WIDTH + KV_WIDTH
IN_WIDTH = FNET_WIDTH + ATTN_WIDTH + 2 * KV_WIDTH
GRID_W = 64
AXIS_DIM = HEAD_DIM // 2
ROPE_THETA = 10000.0
EPS = 1e-6

LANES = 128
BF16_SUBLANES = 16
MXU_DIM = 256
VMEM_LIMIT_BYTES = 60000 * 1024

D_FF_PAD = ((D_FF + MXU_DIM - 1) // MXU_DIM) * MXU_DIM

TOKEN_TILE = 1024
SUB_TILE = 256
ATTN_Q_TILE = 256
WEIGHT_CAST_STEPS = 4
FOURIER_BATCH = 2

BF16 = jnp.bfloat16
F32 = jnp.float32


def _rms_norm(x, gain):
    return x * lax.rsqrt(jnp.mean(x * x, axis=-1, keepdims=True) + EPS) * gain


def _dot(a, b):
    return jnp.dot(a, b, preferred_element_type=F32)


def _swiglu_half_step(x, gain, wg_ref, wu_ref, wd_ref):
    xn = _rms_norm(x, gain).astype(BF16)
    g = _dot(xn, wg_ref[...])
    u = _dot(xn, wu_ref[...])
    a = (g * jax.nn.sigmoid(g) * u).astype(BF16)
    return x + 0.5 * _dot(a, wd_ref[...])


def _split_hi_lo(x):
    hi = x.astype(BF16)
    lo = (x - hi.astype(F32)).astype(BF16)
    return hi, lo


def _ffn1_inproj_kernel(x_ref, g1_ref, wg_ref, wu_ref, wd_ref, gm_ref, win_ref, gqk_ref,
                        cos_ref, sin_ref, seg_ref,
                        x1_ref, uf_ref, q_ref, kp_ref, vt_ref):
    seg = seg_ref[...]
    lane = lax.broadcasted_iota(jnp.int32, (1, LANES), 1)
    upper_half = (lane & (AXIS_DIM // 2)) != 0

    def gate_up(rows):
        xn = _rms_norm(x_ref[rows, :], g1_ref[...]).astype(BF16)
        g = _dot(xn, wg_ref[...])
        return (g * jax.nn.sigmoid(g) * _dot(xn, wu_ref[...])).astype(BF16)

    def down(rows, act):
        x1_ref[rows, :] = x_ref[rows, :] + 0.5 * _dot(act, wd_ref[...])

    def in_proj(rows):
        h = _rms_norm(x1_ref[rows, :], gm_ref[...]).astype(BF16)
        return _dot(h, win_ref[...])

    def heads(rows, u):
        uf_ref[rows, :] = u[:, :FNET_WIDTH].astype(BF16)
        vt = u[:, FNET_WIDTH + QK_WIDTH:].T.astype(BF16)
        for g in range(N_KV_HEADS):
            vt_ref[g, :, rows] = vt[g * HEAD_DIM:(g + 1) * HEAD_DIM]

        cos_t = cos_ref[rows, :]
        sin_t = sin_ref[rows, :]
        for c in range(QK_WIDTH // LANES):
            lo_col = FNET_WIDTH + c * LANES
            z = u[:, lo_col:lo_col + LANES]
            hi, lo = _split_hi_lo(z * z)
            mean_sq = (_dot(hi, seg) + _dot(lo, seg)) * (1.0 / HEAD_DIM)
            zn = z * lax.rsqrt(mean_sq + EPS) * gqk_ref[:, c * LANES:(c + 1) * LANES]
            partner = jnp.where(upper_half,
                                pltpu.roll(zn, AXIS_DIM // 2, axis=1),
                                pltpu.roll(zn, LANES - AXIS_DIM // 2, axis=1))
            out = zn * cos_t + partner * sin_t
            if c < ATTN_WIDTH // LANES:
                q_ref[c // 2, rows, (c % 2) * LANES:(c % 2 + 1) * LANES] = out.astype(BF16)
            else:
                swapped = pltpu.roll(out, HEAD_DIM, axis=1)
                first = lane < HEAD_DIM
                variants = (jnp.where(first, out, 0.0), jnp.where(first, 0.0, swapped),
                            jnp.where(first, swapped, 0.0), jnp.where(first, 0.0, out))
                for i, kv in enumerate(variants):
                    kp_ref[i, rows, :] = kv.astype(BF16)

    sub_tiles = [slice(r * SUB_TILE, (r + 1) * SUB_TILE) for r in range(TOKEN_TILE // SUB_TILE)]
    for slot in range(len(sub_tiles) + 1):
        cur = sub_tiles[slot] if slot < len(sub_tiles) else None
        prev = sub_tiles[slot - 1] if slot > 0 else None
        if cur is not None:
            act = gate_up(cur)
        if prev is not None:
            u_prev = in_proj(prev)
        if cur is not None:
            down(cur, act)
        if prev is not None:
            heads(prev, u_prev)


def _fourier_kernel(uf_ref, ch_ref, sh_ref, rev_ref, cc_ref, sc_ref, fw_ref, fb_ref, out_ref, e1_ref, o2_ref):
    n_batch, n = uf_ref.shape[0], uf_ref.shape[1]
    h = n // 2
    rev = rev_ref[...]

    cc_hi, cc_lo = _split_hi_lo(cc_ref[...])
    sc_hi, sc_lo = _split_hi_lo(sc_ref[...])
    folded = []
    for g in range(FNET_GROUPS):
        w_hi, w_lo = _split_hi_lo(fw_ref[g])
        folded.append(((_dot(cc_hi, w_hi) + _dot(cc_hi, w_lo) + _dot(cc_lo, w_hi)).astype(BF16),
                       (_dot(sc_hi, w_hi) + _dot(sc_hi, w_lo) + _dot(sc_lo, w_hi)).astype(BF16)))

    def mirror(b):
        mirrored = _dot(rev, uf_ref[b, h:, :])
        lo32 = uf_ref[b, :h, :].astype(F32)
        return (lo32 + mirrored).astype(BF16), (lo32 - mirrored).astype(BF16)

    def channel_mix(b, even, odd):
        mid_rows = []
        for g, (wc, ws) in enumerate(folded):
            cols = slice(g * FNET_GROUP_DIM, (g + 1) * FNET_GROUP_DIM)
            e1_ref[b, :, cols] = _dot(even[:, cols], wc).astype(BF16)
            o2_ref[b, :, cols] = _dot(odd[:, cols], ws).astype(BF16)
            mid_rows.append(_dot(uf_ref[b, h:h + BF16_SUBLANES, cols], wc)[0:1])
        return jnp.concatenate(mid_rows, axis=-1)

    def position_dft(b, mid):
        a = _dot(ch_ref[...], e1_ref[b])
        s = _dot(sh_ref[...], o2_ref[b])
        j = lax.broadcasted_iota(jnp.int32, (a.shape[0], 1), 0)
        a = a + jnp.where((j & 1) == 0, 1.0, -1.0) * mid + fb_ref[...]
        out_ref[b, :h, :] = (a[:h] - s[:h]).astype(BF16)
        return (a + s).astype(BF16)

    def mirror_out(b, z):
        out_ref[b, h:, :] = _dot(rev, z[:h]).astype(BF16)
        out_ref[b, h:h + 1, :] = z[h:h + 1]

    halves = [mirror(b) for b in range(n_batch)]
    mids = [channel_mix(b, *halves[b]) for b in range(n_batch)]
    zs = [position_dft(b, mids[b]) for b in range(n_batch)]
    for b in range(n_batch):
        mirror_out(b, zs[b])


def _attention_kernel(q_ref, kp_ref, vt_ref, o_ref, s_ref, p_ref, m_ref, att_ref):
    seq = kp_ref.shape[1]
    tq = ATTN_Q_TILE
    n_tiles = seq // tq
    n_streams = 2 * N_KV_HEADS
    ones_rows = jnp.ones((BF16_SUBLANES, seq), BF16)

    def q_rows(t):
        start = t * tq
        if not isinstance(start, int):
            start = pl.multiple_of(start, tq)
        return pl.ds(start, tq)

    def qk(t, r):
        g = r // 2
        qblk = q_ref[g, q_rows(t), :]
        qg = jnp.concatenate([qblk[:, :LANES], qblk[:, LANES:]], axis=0)
        s = lax.dot_general(kp_ref[r], qg, (((1,), (1,)), ((), ())), preferred_element_type=F32)
        s_ref[r] = s
        m_ref[r, t % 2] = jnp.max(s, axis=0, keepdims=True)

    def ex(t, r):
        p_ref[r] = jnp.exp2(s_ref[r] - m_ref[r, t % 2]).astype(BF16)

    def pv(r):
        v1t = jnp.concatenate([vt_ref[r // 2], ones_rows], axis=0)
        o = _dot(v1t, p_ref[r])
        att_ref[r] = o[:HEAD_DIM] * (1.0 / o[HEAD_DIM:HEAD_DIM + 1])

    def emit(t, g):
        for c in range(2):
            pair = jnp.concatenate([att_ref[2 * g + h, :, c * tq:(c + 1) * tq] for h in range(2)], axis=0)
            o_ref[g, q_rows(t), c * LANES:(c + 1) * LANES] = pair.T.astype(BF16)

    def tile(t, first=False, last=False):
        for r in range(n_streams):
            ex(t, r)
            if not last:
                qk(t + 1, r)
            if r > 0:
                pv(r - 1)
                if r % 2 == 0:
                    emit(t, r // 2 - 1)
            elif not first:
                pv(n_streams - 1)
                emit(t - 1, N_KV_HEADS - 1)

    def loop_body(t, carry):
        tile(t)
        return carry

    for r in range(n_streams):
        qk(0, r)
    tile(0, first=True)
    lax.fori_loop(1, n_tiles - 1, loop_body, 0)
    tile(n_tiles - 1, last=True)
    pv(n_streams - 1)
    emit(n_tiles - 1, N_KV_HEADS - 1)


def _outproj_ffn2_kernel(x1_ref, f_ref, a_ref, wof_ref, woa_ref, g2_ref, wg_ref, wu_ref, wd_ref,
                         gf_ref, y_ref):
    sub_tiles = [slice(r * SUB_TILE, (r + 1) * SUB_TILE) for r in range(TOKEN_TILE // SUB_TILE)]
    for rows in sub_tiles:
        x2 = x1_ref[rows, :] + _dot(f_ref[rows, :], wof_ref[...])
        for g in range(N_KV_HEADS):
            x2 = x2 + _dot(a_ref[g, rows, :], woa_ref[g])
        y_ref[rows, :] = x2
    for rows in sub_tiles:
        x3 = _swiglu_half_step(y_ref[rows, :], g2_ref[...], wg_ref, wu_ref, wd_ref)
        y_ref[rows, :] = _rms_norm(x3, gf_ref[...])


def _resident(shape):
    return pl.BlockSpec(shape, lambda *_: (0,) * len(shape), pipeline_mode=pl.Buffered(1))


def _rope_tables(seq):
    rows = seq // GRID_W
    row_idx = np.repeat(np.arange(rows, dtype=np.float64), GRID_W)
    col_idx = np.tile(np.arange(GRID_W, dtype=np.float64), rows)
    inv_freq = ROPE_THETA ** (-np.arange(0, AXIS_DIM, 2, dtype=np.float64) / AXIS_DIM)
    ang_r = row_idx[:, None] * inv_freq[None, :]
    ang_c = col_idx[:, None] * inv_freq[None, :]
    cos_h = np.concatenate([np.cos(ang_r)] * 2 + [np.cos(ang_c)] * 2, axis=-1)
    sin_h = np.concatenate([-np.sin(ang_r), np.sin(ang_r), -np.sin(ang_c), np.sin(ang_c)], axis=-1)
    reps = LANES // HEAD_DIM
    return (np.tile(cos_h, (1, reps)).astype(np.float32), np.tile(sin_h, (1, reps)).astype(np.float32))


def _dft_cos_sin(n, rows, cols):
    j = np.arange(rows, dtype=np.int64)
    k = np.arange(cols, dtype=np.int64)
    ang = ((j[:, None] * k[None, :]) % n).astype(np.float64) * (2.0 * math.pi / n)
    return np.cos(ang).astype(np.float32), np.sin(ang).astype(np.float32)


def _cast_pad_cols_kernel(*refs):
    n = len(refs) // 2
    aligned = (D_FF // LANES) * LANES
    for wt_ref, o_ref in zip(refs[:n], refs[n:]):
        o_ref[:, :aligned] = wt_ref[:aligned, :].T.astype(BF16)
        tail = jnp.concatenate([wt_ref[aligned:, :], jnp.zeros((D_FF_PAD - D_FF, wt_ref.shape[1]), F32)], axis=0)
        o_ref[:, aligned:] = tail.T.astype(BF16)


def _cast_pad_rows_kernel(*refs):
    n = len(refs) // 2
    for w_ref, o_ref in zip(refs[:n], refs[n:]):
        o_ref[:D_FF, :] = w_ref[...].astype(BF16)
        o_ref[D_FF:, :] = jnp.zeros((D_FF_PAD - D_FF, o_ref.shape[1]), BF16)


def _cast_kernel(*refs):
    n = len(refs) // 2
    for w_ref, o_ref in zip(refs[:n], refs[n:]):
        o_ref[...] = w_ref[...].astype(BF16)


def _cast_weights(kernel_fn, weights, in_block, in_map, out_shape, out_block, out_map, steps, name):
    return pl.pallas_call(
        kernel_fn,
        grid=(steps,),
        in_specs=[pl.BlockSpec(in_block, in_map)] * len(weights),
        out_specs=[pl.BlockSpec(out_block, out_map)] * len(weights),
        out_shape=[jax.ShapeDtypeStruct(out_shape, BF16)] * len(weights),
        compiler_params=pltpu.CompilerParams(dimension_semantics=("arbitrary",),
                                             vmem_limit_bytes=VMEM_LIMIT_BYTES),
        name=name,
    )(*weights)


def _row(v):
    return v.reshape(1, -1).astype(F32)


def kernel(x, ffn1_norm, ffn1_w_gate, ffn1_w_up, ffn1_w_down, mix_norm, w_in, fnet_w, fnet_b, q_norm,
           k_norm, w_out, ffn2_norm, ffn2_w_gate, ffn2_w_up, ffn2_w_down, final_norm):
    batch, seq, d = x.shape
    assert d == D_MODEL and seq % TOKEN_TILE == 0 and seq % ATTN_Q_TILE == 0 and seq % GRID_W == 0
    n_tok = batch * seq
    tiles_per_seq = seq // TOKEN_TILE
    x2d = x.reshape(n_tok, d)

    params = pltpu.CompilerParams(dimension_semantics=("arbitrary",), vmem_limit_bytes=VMEM_LIMIT_BYTES)
    tok = lambda width: pl.BlockSpec((TOKEN_TILE, width), lambda i: (i, 0))

    blk = d // WEIGHT_CAST_STEPS
    by_rows, by_cols = (lambda i: (i, 0)), (lambda i: (0, i))
    wg1, wu1, wg2, wu2 = _cast_weights(
        _cast_pad_cols_kernel, [w.T for w in (ffn1_w_gate, ffn1_w_up, ffn2_w_gate, ffn2_w_up)],
        (D_FF, blk), by_cols, (d, D_FF_PAD), (blk, D_FF_PAD), by_rows, WEIGHT_CAST_STEPS, "cast_gate_up")
    wd1, wd2 = _cast_weights(
        _cast_pad_rows_kernel, [ffn1_w_down, ffn2_w_down],
        (D_FF, blk), by_cols, (D_FF_PAD, d), (D_FF_PAD, blk), by_cols, WEIGHT_CAST_STEPS, "cast_down")
    w_in_bf, = _cast_weights(_cast_kernel, [w_in], (blk, IN_WIDTH), by_rows, (d, IN_WIDTH), (blk, IN_WIDTH),
                             by_rows, WEIGHT_CAST_STEPS, "cast_w_in")
    w_out_bf, = _cast_weights(_cast_kernel, [w_out], (blk, d), by_rows, (d, d), (blk, d), by_rows,
                              WEIGHT_CAST_STEPS, "cast_w_out")

    cos_t, sin_t = _rope_tables(seq)
    scale = math.log2(math.e) / math.sqrt(HEAD_DIM)
    gain_qk = jnp.concatenate([jnp.tile(q_norm.astype(F32) * scale, N_Q_HEADS),
                               jnp.tile(k_norm.astype(F32), N_KV_HEADS)]).reshape(1, QK_WIDTH)
    head_of_lane = np.arange(LANES) // HEAD_DIM
    seg_ones = jnp.asarray(head_of_lane[:, None] == head_of_lane[None, :], dtype=BF16)
    rope_spec = pl.BlockSpec((TOKEN_TILE, LANES), lambda i: (i % tiles_per_seq, 0))

    grp = lambda n, width: pl.BlockSpec((n, TOKEN_TILE, width), lambda i: (0, i, 0))
    vt_spec = pl.BlockSpec((None, N_KV_HEADS, HEAD_DIM, TOKEN_TILE),
                           lambda i: (i // tiles_per_seq, 0, 0, i % tiles_per_seq))
    x1, uf, q, kp, vt = pl.pallas_call(
        _ffn1_inproj_kernel,
        grid=(n_tok // TOKEN_TILE,),
        in_specs=[tok(d), _resident((1, d)),
                  _resident((d, D_FF_PAD)), _resident((d, D_FF_PAD)), _resident((D_FF_PAD, d)),
                  _resident((1, d)), _resident((d, IN_WIDTH)), _resident((1, QK_WIDTH)),
                  rope_spec, rope_spec, _resident((LANES, LANES))],
        out_specs=[tok(d), tok(FNET_WIDTH), grp(N_KV_HEADS, 2 * LANES), grp(2 * N_KV_HEADS, LANES), vt_spec],
        out_shape=[jax.ShapeDtypeStruct((n_tok, d), F32),
                   jax.ShapeDtypeStruct((n_tok, FNET_WIDTH), BF16),
                   jax.ShapeDtypeStruct((N_KV_HEADS, n_tok, 2 * LANES), BF16),
                   jax.ShapeDtypeStruct((2 * N_KV_HEADS, n_tok, LANES), BF16),
                   jax.ShapeDtypeStruct((batch, N_KV_HEADS, HEAD_DIM, seq), BF16)],
        compiler_params=params,
        name="ffn1_inproj",
    )(x2d, _row(ffn1_norm), wg1, wu1, wd1, _row(mix_norm), w_in_bf, gain_qk, cos_t, sin_t, seg_ones)

    half = seq // 2
    dft_rows = half + BF16_SUBLANES
    cos_s, sin_s = _dft_cos_sin(seq, dft_rows, half)
    idx = np.arange(half)
    rev = jnp.asarray((idx[:, None] + idx[None, :]) == half, dtype=BF16)
    cos_c, sin_c = _dft_cos_sin(FNET_GROUP_DIM, FNET_GROUP_DIM, FNET_GROUP_DIM)
    ortho = 1.0 / math.sqrt(seq * FNET_GROUP_DIM)
    f_out = pl.pallas_call(
        _fourier_kernel,
        grid=(batch // FOURIER_BATCH,),
        in_specs=[pl.BlockSpec((FOURIER_BATCH, seq, FNET_WIDTH), lambda b: (b, 0, 0)),
                  _resident((dft_rows, half)), _resident((dft_rows, half)), _resident((half, half)),
                  _resident((FNET_GROUP_DIM, FNET_GROUP_DIM)), _resident((FNET_GROUP_DIM, FNET_GROUP_DIM)),
                  _resident((FNET_GROUPS, FNET_GROUP_DIM, FNET_GROUP_DIM)), _resident((1, FNET_WIDTH))],
        out_specs=pl.BlockSpec((FOURIER_BATCH, seq, FNET_WIDTH), lambda b: (b, 0, 0)),
        out_shape=jax.ShapeDtypeStruct((batch, seq, FNET_WIDTH), BF16),
        scratch_shapes=[pltpu.VMEM((FOURIER_BATCH, half, FNET_WIDTH), BF16),
                        pltpu.VMEM((FOURIER_BATCH, half, FNET_WIDTH), BF16)],
        compiler_params=params,
        name="fourier",
    )(uf.reshape(batch, seq, FNET_WIDTH), jnp.asarray(cos_s).astype(BF16), jnp.asarray(sin_s).astype(BF16), rev,
      cos_c * ortho, sin_c * ortho, fnet_w.astype(F32), fnet_b.reshape(1, FNET_WIDTH).astype(F32))

    a_out = pl.pallas_call(
        _attention_kernel,
        grid=(batch,),
        in_specs=[pl.BlockSpec((N_KV_HEADS, seq, 2 * LANES), lambda b: (0, b, 0)),
                  pl.BlockSpec((2 * N_KV_HEADS, seq, LANES), lambda b: (0, b, 0)),
                  pl.BlockSpec((None, N_KV_HEADS, HEAD_DIM, seq), lambda b: (b, 0, 0, 0))],
        out_specs=pl.BlockSpec((N_KV_HEADS, seq, 2 * LANES), lambda b: (0, b, 0)),
        out_shape=jax.ShapeDtypeStruct((N_KV_HEADS, n_tok, 2 * LANES), BF16),
        scratch_shapes=[pltpu.VMEM((2 * N_KV_HEADS, seq, 2 * ATTN_Q_TILE), F32),
                        pltpu.VMEM((2 * N_KV_HEADS, seq, 2 * ATTN_Q_TILE), BF16),
                        pltpu.VMEM((2 * N_KV_HEADS, 2, 1, 2 * ATTN_Q_TILE), F32),
                        pltpu.VMEM((2 * N_KV_HEADS, HEAD_DIM, 2 * ATTN_Q_TILE), F32)],
        compiler_params=params,
        name="attention",
    )(q, kp, vt)

    y = pl.pallas_call(
        _outproj_ffn2_kernel,
        grid=(n_tok // TOKEN_TILE,),
        in_specs=[tok(d), tok(FNET_WIDTH), grp(N_KV_HEADS, 2 * LANES),
                  _resident((FNET_WIDTH, d)), _resident((N_KV_HEADS, 2 * LANES, d)), _resident((1, d)),
                  _resident((d, D_FF_PAD)), _resident((d, D_FF_PAD)), _resident((D_FF_PAD, d)),
                  _resident((1, d))],
        out_specs=tok(d),
        out_shape=jax.ShapeDtypeStruct((n_tok, d), F32),
        compiler_params=params,
        name="outproj_ffn2",
    )(x1, f_out.reshape(n_tok, FNET_WIDTH), a_out,
      w_out_bf[:FNET_WIDTH], w_out_bf[FNET_WIDTH:].reshape(N_KV_HEADS, 2 * LANES, d), _row(ffn2_norm),
      wg2, wu2, wd2, _row(final_norm))
    return y.reshape(batch, seq, d)
```

```python
import math

import numpy as np
import jax
import jax.numpy as jnp
from jax import lax
from jax.experimental import pallas as pl
from jax.experimental.pallas import tpu as pltpu

D_MODEL = 1024
D_FF = 2752
FNET_WIDTH = 512
FNET_GROUPS = 4
FNET_GROUP_DIM = 128
HEAD_DIM = 64
N_Q_HEADS = 8
N_KV_HEADS = 2
ATTN_WIDTH = N_Q_HEADS * HEAD_DIM
KV_WIDTH = N_KV_HEADS * HEAD_DIM
QK_WIDTH = ATTN_WIDTH + KV_WIDTH
IN_WIDTH = FNET_WIDTH + ATTN_WIDTH + 2 * KV_WIDTH
GRID_W = 64
AXIS_DIM = HEAD_DIM // 2
ROPE_THETA = 10000.0
EPS = 1e-6

LANES = 128
BF16_SUBLANES = 16
MXU_DIM = 256
VMEM_LIMIT_BYTES = 60000 * 1024

D_FF_PAD = ((D_FF + MXU_DIM - 1) // MXU_DIM) * MXU_DIM

TOKEN_TILE = 1024
SUB_TILE = 256
ATTN_Q_TILE = 256
WEIGHT_CAST_STEPS = 4
FOURIER_BATCH = 2

BF16 = jnp.bfloat16
F32 = jnp.float32


def _rms_norm(x, gain):
    return x * lax.rsqrt(jnp.mean(x * x, axis=-1, keepdims=True) + EPS) * gain


def _dot(a, b):
    return jnp.dot(a, b, preferred_element_type=F32)


def _swiglu_half_step(x, gain, wg_ref, wu_ref, wd_ref):
    xn = _rms_norm(x, gain).astype(BF16)
    g = _dot(xn, wg_ref[...])
    u = _dot(xn, wu_ref[...])
    a = (g * jax.nn.sigmoid(g) * u).astype(BF16)
    return x + 0.5 * _dot(a, wd_ref[...])


def _split_hi_lo(x):
    hi = x.astype(BF16)
    lo = (x - hi.astype(F32)).astype(BF16)
    return hi, lo


def _ffn1_inproj_kernel(x_ref, g1_ref, wg_ref, wu_ref, wd_ref, gm_ref, win_ref, gqk_ref,
                        cos_ref, sin_ref, seg_ref,
                        x1_ref, uf_ref, q_ref, kp_ref, vt_ref):
    seg = seg_ref[...]
    lane = lax.broadcasted_iota(jnp.int32, (1, LANES), 1)
    upper_half = (lane & (AXIS_DIM // 2)) != 0

    def gate_up(rows):
        xn = _rms_norm(x_ref[rows, :], g1_ref[...]).astype(BF16)
        g = _dot(xn, wg_ref[...])
        return (g * jax.nn.sigmoid(g) * _dot(xn, wu_ref[...])).astype(BF16)

    def down(rows, act):
        x1_ref[rows, :] = x_ref[rows, :] + 0.5 * _dot(act, wd_ref[...])

    def in_proj(rows):
        h = _rms_norm(x1_ref[rows, :], gm_ref[...]).astype(BF16)
        return _dot(h, win_ref[...])

    def heads(rows, u):
        uf_ref[rows, :] = u[:, :FNET_WIDTH].astype(BF16)
        vt = u[:, FNET_WIDTH + QK_WIDTH:].T.astype(BF16)
        for g in range(N_KV_HEADS):
            vt_ref[g, :, rows] = vt[g * HEAD_DIM:(g + 1) * HEAD_DIM]

        cos_t = cos_ref[rows, :]
        sin_t = sin_ref[rows, :]
        for c in range(QK_WIDTH // LANES):
            lo_col = FNET_WIDTH + c * LANES
            z = u[:, lo_col:lo_col + LANES]
            hi, lo = _split_hi_lo(z * z)
            mean_sq = (_dot(hi, seg) + _dot(lo, seg)) * (1.0 / HEAD_DIM)
            zn = z * lax.rsqrt(mean_sq + EPS) * gqk_ref[:, c * LANES:(c + 1) * LANES]
            partner = jnp.where(upper_half,
                                pltpu.roll(zn, AXIS_DIM // 2, axis=1),
                                pltpu.roll(zn, LANES - AXIS_DIM // 2, axis=1))
            out = zn * cos_t + partner * sin_t
            if c < ATTN_WIDTH // LANES:
                q_ref[c // 2, rows, (c % 2) * LANES:(c % 2 + 1) * LANES] = out.astype(BF16)
            else:
                swapped = pltpu.roll(out, HEAD_DIM, axis=1)
                first = lane < HEAD_DIM
                variants = (jnp.where(first, out, 0.0), jnp.where(first, 0.0, swapped),
                            jnp.where(first, swapped, 0.0), jnp.where(first, 0.0, out))
                for i, kv in enumerate(variants):
                    kp_ref[i, rows, :] = kv.astype(BF16)

    sub_tiles = [slice(r * SUB_TILE, (r + 1) * SUB_TILE) for r in range(TOKEN_TILE // SUB_TILE)]
    for slot in range(len(sub_tiles) + 1):
        cur = sub_tiles[slot] if slot < len(sub_tiles) else None
        prev = sub_tiles[slot - 1] if slot > 0 else None
        if cur is not None:
            act = gate_up(cur)
        if prev is not None:
            u_prev = in_proj(prev)
        if cur is not None:
            down(cur, act)
        if prev is not None:
            heads(prev, u_prev)


def _fourier_kernel(uf_ref, ch_ref, sh_ref, rev_ref, cc_ref, sc_ref, fw_ref, fb_ref,
                    wgt_ref, wut_ref, wd_ref, wo_ref,
                    out_ref, wg_out_ref, wu_out_ref, wd_out_ref, wo_out_ref, e1_ref, o2_ref):
    n_batch, n = uf_ref.shape[0], uf_ref.shape[1]
    h = n // 2
    rev = rev_ref[...]

    cc_hi, cc_lo = _split_hi_lo(cc_ref[...])
    sc_hi, sc_lo = _split_hi_lo(sc_ref[...])
    folded = []
    for g in range(FNET_GROUPS):
        w_hi, w_lo = _split_hi_lo(fw_ref[g])
        folded.append(((_dot(cc_hi, w_hi) + _dot(cc_hi, w_lo) + _dot(cc_lo, w_hi)).astype(BF16),
                       (_dot(sc_hi, w_hi) + _dot(sc_hi, w_lo) + _dot(sc_lo, w_hi)).astype(BF16)))

    def mirror(b):
        mirrored = _dot(rev, uf_ref[b, h:, :])
        lo32 = uf_ref[b, :h, :].astype(F32)
        return (lo32 + mirrored).astype(BF16), (lo32 - mirrored).astype(BF16)

    def channel_mix(b, even, odd):
        mid_rows = []
        for g, (wc, ws) in enumerate(folded):
            cols = slice(g * FNET_GROUP_DIM, (g + 1) * FNET_GROUP_DIM)
            e1_ref[b, :, cols] = _dot(even[:, cols], wc).astype(BF16)
            o2_ref[b, :, cols] = _dot(odd[:, cols], ws).astype(BF16)
            mid_rows.append(_dot(uf_ref[b, h:h + BF16_SUBLANES, cols], wc)[0:1])
        return jnp.concatenate(mid_rows, axis=-1)

    def position_dft(b, mid):
        a = _dot(ch_ref[...], e1_ref[b])
        s = _dot(sh_ref[...], o2_ref[b])
        j = lax.broadcasted_iota(jnp.int32, (a.shape[0], 1), 0)
        a = a + jnp.where((j & 1) == 0, 1.0, -1.0) * mid + fb_ref[...]
        out_ref[b, :h, :] = (a[:h] - s[:h]).astype(BF16)
        return (a + s).astype(BF16)

    def mirror_out(b, z):
        out_ref[b, h:, :] = _dot(rev, z[:h]).astype(BF16)
        out_ref[b, h:h + 1, :] = z[h:h + 1]

    halves = [mirror(b) for b in range(n_batch)]
    mids = [channel_mix(b, *halves[b]) for b in range(n_batch)]
    zs = [position_dft(b, mids[b]) for b in range(n_batch)]
    for b in range(n_batch):
        mirror_out(b, zs[b])

    _cast_pad_cols_kernel(wgt_ref, wut_ref, wg_out_ref, wu_out_ref)
    _cast_pad_rows_kernel(wd_ref, wd_out_ref)
    _cast_kernel(wo_ref, wo_out_ref)


def _attention_kernel(q_ref, kp_ref, vt_ref, o_ref, s_ref, p_ref, m_ref, att_ref):
    seq = kp_ref.shape[1]
    tq = ATTN_Q_TILE
    n_tiles = seq // tq
    n_streams = 2 * N_KV_HEADS
    ones_rows = jnp.ones((BF16_SUBLANES, seq), BF16)

    def q_rows(t):
        start = t * tq
        if not isinstance(start, int):
            start = pl.multiple_of(start, tq)
        return pl.ds(start, tq)

    def qk(t, r):
        g = r // 2
        qblk = q_ref[g, q_rows(t), :]
        qg = jnp.concatenate([qblk[:, :LANES], qblk[:, LANES:]], axis=0)
        s = lax.dot_general(kp_ref[r], qg, (((1,), (1,)), ((), ())), preferred_element_type=F32)
        s_ref[r] = s
        m_ref[r, t % 2] = jnp.max(s, axis=0, keepdims=True)

    def ex(t, r):
        p_ref[r] = jnp.exp2(s_ref[r] - m_ref[r, t % 2]).astype(BF16)

    def pv(r):
        v1t = jnp.concatenate([vt_ref[r // 2], ones_rows], axis=0)
        o = _dot(v1t, p_ref[r])
        att_ref[r] = o[:HEAD_DIM] * (1.0 / o[HEAD_DIM:HEAD_DIM + 1])

    def emit(t, g):
        for c in range(2):
            pair = jnp.concatenate([att_ref[2 * g + h, :, c * tq:(c + 1) * tq] for h in range(2)], axis=0)
            o_ref[g, q_rows(t), c * LANES:(c + 1) * LANES] = pair.T.astype(BF16)

    def tile(t, first=False, last=False):
        for r in range(n_streams):
            ex(t, r)
            if not last:
                qk(t + 1, r)
            if r > 0:
                pv(r - 1)
                if r % 2 == 0:
                    emit(t, r // 2 - 1)
            elif not first:
                pv(n_streams - 1)
                emit(t - 1, N_KV_HEADS - 1)

    def loop_body(t, carry):
        tile(t)
        return carry

    for r in range(n_streams):
        qk(0, r)
    tile(0, first=True)
    lax.fori_loop(1, n_tiles - 1, loop_body, 0)
    tile(n_tiles - 1, last=True)
    pv(n_streams - 1)
    emit(n_tiles - 1, N_KV_HEADS - 1)


def _outproj_ffn2_kernel(x1_ref, f_ref, a_ref, wof_ref, woa_ref, g2_ref, wg_ref, wu_ref, wd_ref,
                         gf_ref, y_ref):
    sub_tiles = [slice(r * SUB_TILE, (r + 1) * SUB_TILE) for r in range(TOKEN_TILE // SUB_TILE)]
    for rows in sub_tiles:
        x2 = x1_ref[rows, :] + _dot(f_ref[rows, :], wof_ref[...])
        for g in range(N_KV_HEADS):
            x2 = x2 + _dot(a_ref[g, rows, :], woa_ref[g])
        y_ref[rows, :] = x2
    for rows in sub_tiles:
        x3 = _swiglu_half_step(y_ref[rows, :], g2_ref[...], wg_ref, wu_ref, wd_ref)
        y_ref[rows, :] = _rms_norm(x3, gf_ref[...])


def _resident(shape):
    return pl.BlockSpec(shape, lambda *_: (0,) * len(shape), pipeline_mode=pl.Buffered(1))


def _rope_tables(seq):
    rows = seq // GRID_W
    row_idx = np.repeat(np.arange(rows, dtype=np.float64), GRID_W)
    col_idx = np.tile(np.arange(GRID_W, dtype=np.float64), rows)
    inv_freq = ROPE_THETA ** (-np.arange(0, AXIS_DIM, 2, dtype=np.float64) / AXIS_DIM)
    ang_r = row_idx[:, None] * inv_freq[None, :]
    ang_c = col_idx[:, None] * inv_freq[None, :]
    cos_h = np.concatenate([np.cos(ang_r)] * 2 + [np.cos(ang_c)] * 2, axis=-1)
    sin_h = np.concatenate([-np.sin(ang_r), np.sin(ang_r), -np.sin(ang_c), np.sin(ang_c)], axis=-1)
    reps = LANES // HEAD_DIM
    return (np.tile(cos_h, (1, reps)).astype(np.float32), np.tile(sin_h, (1, reps)).astype(np.float32))


def _dft_cos_sin(n, rows, cols):
    j = np.arange(rows, dtype=np.int64)
    k = np.arange(cols, dtype=np.int64)
    ang = ((j[:, None] * k[None, :]) % n).astype(np.float64) * (2.0 * math.pi / n)
    return np.cos(ang).astype(np.float32), np.sin(ang).astype(np.float32)


def _cast_pad_cols_kernel(*refs):
    n = len(refs) // 2
    aligned = (D_FF // LANES) * LANES
    for wt_ref, o_ref in zip(refs[:n], refs[n:]):
        o_ref[:, :aligned] = wt_ref[:aligned, :].T.astype(BF16)
        tail = jnp.concatenate([wt_ref[aligned:, :], jnp.zeros((D_FF_PAD - D_FF, wt_ref.shape[1]), F32)], axis=0)
        o_ref[:, aligned:] = tail.T.astype(BF16)


def _cast_pad_rows_kernel(*refs):
    n = len(refs) // 2
    for w_ref, o_ref in zip(refs[:n], refs[n:]):
        o_ref[:D_FF, :] = w_ref[...].astype(BF16)
        o_ref[D_FF:, :] = jnp.zeros((D_FF_PAD - D_FF, o_ref.shape[1]), BF16)


def _cast_kernel(*refs):
    n = len(refs) // 2
    for w_ref, o_ref in zip(refs[:n], refs[n:]):
        o_ref[...] = w_ref[...].astype(BF16)


def _cast_weights(kernel_fn, weights, in_block, in_map, out_shape, out_block, out_map, steps, name):
    return pl.pallas_call(
        kernel_fn,
        grid=(steps,),
        in_specs=[pl.BlockSpec(in_block, in_map)] * len(weights),
        out_specs=[pl.BlockSpec(out_block, out_map)] * len(weights),
        out_shape=[jax.ShapeDtypeStruct(out_shape, BF16)] * len(weights),
        compiler_params=pltpu.CompilerParams(dimension_semantics=("arbitrary",),
                                             vmem_limit_bytes=VMEM_LIMIT_BYTES),
        name=name,
    )(*weights)


def _row(v):
    return v.reshape(1, -1).astype(F32)


def kernel(x, ffn1_norm, ffn1_w_gate, ffn1_w_up, ffn1_w_down, mix_norm, w_in, fnet_w, fnet_b, q_norm,
           k_norm, w_out, ffn2_norm, ffn2_w_gate, ffn2_w_up, ffn2_w_down, final_norm):
    batch, seq, d = x.shape
    assert d == D_MODEL and seq % TOKEN_TILE == 0 and seq % ATTN_Q_TILE == 0 and seq % GRID_W == 0
    n_tok = batch * seq
    tiles_per_seq = seq // TOKEN_TILE
    x2d = x.reshape(n_tok, d)

    params = pltpu.CompilerParams(dimension_semantics=("arbitrary",), vmem_limit_bytes=VMEM_LIMIT_BYTES)
    tok = lambda width: pl.BlockSpec((TOKEN_TILE, width), lambda i: (i, 0))

    blk = d // WEIGHT_CAST_STEPS
    by_rows, by_cols = (lambda i: (i, 0)), (lambda i: (0, i))
    wg1, wu1 = _cast_weights(
        _cast_pad_cols_kernel, [ffn1_w_gate.T, ffn1_w_up.T],
        (D_FF, blk), by_cols, (d, D_FF_PAD), (blk, D_FF_PAD), by_rows, WEIGHT_CAST_STEPS, "cast_gate_up")
    wd1, = _cast_weights(
        _cast_pad_rows_kernel, [ffn1_w_down],
        (D_FF, blk), by_cols, (D_FF_PAD, d), (D_FF_PAD, blk), by_cols, WEIGHT_CAST_STEPS, "cast_down")
    w_in_bf, = _cast_weights(_cast_kernel, [w_in], (blk, IN_WIDTH), by_rows, (d, IN_WIDTH), (blk, IN_WIDTH),
                             by_rows, WEIGHT_CAST_STEPS, "cast_w_in")

    cos_t, sin_t = _rope_tables(seq)
    scale = math.log2(math.e) / math.sqrt(HEAD_DIM)
    gain_qk = jnp.concatenate([jnp.tile(q_norm.astype(F32) * scale, N_Q_HEADS),
                               jnp.tile(k_norm.astype(F32), N_KV_HEADS)]).reshape(1, QK_WIDTH)
    head_of_lane = np.arange(LANES) // HEAD_DIM
    seg_ones = jnp.asarray(head_of_lane[:, None] == head_of_lane[None, :], dtype=BF16)
    rope_spec = pl.BlockSpec((TOKEN_TILE, LANES), lambda i: (i % tiles_per_seq, 0))

    grp = lambda n, width: pl.BlockSpec((n, TOKEN_TILE, width), lambda i: (0, i, 0))
    vt_spec = pl.BlockSpec((None, N_KV_HEADS, HEAD_DIM, TOKEN_TILE),
                           lambda i: (i // tiles_per_seq, 0, 0, i % tiles_per_seq))
    x1, uf, q, kp, vt = pl.pallas_call(
        _ffn1_inproj_kernel,
        grid=(n_tok // TOKEN_TILE,),
        in_specs=[tok(d), _resident((1, d)),
                  _resident((d, D_FF_PAD)), _resident((d, D_FF_PAD)), _resident((D_FF_PAD, d)),
                  _resident((1, d)), _resident((d, IN_WIDTH)), _resident((1, QK_WIDTH)),
                  rope_spec, rope_spec, _resident((LANES, LANES))],
        out_specs=[tok(d), tok(FNET_WIDTH), grp(N_KV_HEADS, 2 * LANES), grp(2 * N_KV_HEADS, LANES), vt_spec],
        out_shape=[jax.ShapeDtypeStruct((n_tok, d), F32),
                   jax.ShapeDtypeStruct((n_tok, FNET_WIDTH), BF16),
                   jax.ShapeDtypeStruct((N_KV_HEADS, n_tok, 2 * LANES), BF16),
                   jax.ShapeDtypeStruct((2 * N_KV_HEADS, n_tok, LANES), BF16),
                   jax.ShapeDtypeStruct((batch, N_KV_HEADS, HEAD_DIM, seq), BF16)],
        compiler_params=params,
        name="ffn1_inproj",
    )(x2d, _row(ffn1_norm), wg1, wu1, wd1, _row(mix_norm), w_in_bf, gain_qk, cos_t, sin_t, seg_ones)

    half = seq // 2
    dft_rows = half + BF16_SUBLANES
    cos_s, sin_s = _dft_cos_sin(seq, dft_rows, half)
    idx = np.arange(half)
    rev = jnp.asarray((idx[:, None] + idx[None, :]) == half, dtype=BF16)
    cos_c, sin_c = _dft_cos_sin(FNET_GROUP_DIM, FNET_GROUP_DIM, FNET_GROUP_DIM)
    ortho = 1.0 / math.sqrt(seq * FNET_GROUP_DIM)
    fourier_steps = batch // FOURIER_BATCH
    wblk = d // fourier_steps
    assert d % fourier_steps == 0 and wblk % LANES == 0
    f_out, wg2, wu2, wd2, w_out_bf = pl.pallas_call(
        _fourier_kernel,
        grid=(fourier_steps,),
        in_specs=[pl.BlockSpec((FOURIER_BATCH, seq, FNET_WIDTH), lambda b: (b, 0, 0)),
                  _resident((dft_rows, half)), _resident((dft_rows, half)), _resident((half, half)),
                  _resident((FNET_GROUP_DIM, FNET_GROUP_DIM)), _resident((FNET_GROUP_DIM, FNET_GROUP_DIM)),
                  _resident((FNET_GROUPS, FNET_GROUP_DIM, FNET_GROUP_DIM)), _resident((1, FNET_WIDTH)),
                  pl.BlockSpec((D_FF, wblk), by_cols), pl.BlockSpec((D_FF, wblk), by_cols),
                  pl.BlockSpec((D_FF, wblk), by_cols), pl.BlockSpec((wblk, d), by_rows)],
        out_specs=[pl.BlockSpec((FOURIER_BATCH, seq, FNET_WIDTH), lambda b: (b, 0, 0)),
                   pl.BlockSpec((wblk, D_FF_PAD), by_rows), pl.BlockSpec((wblk, D_FF_PAD), by_rows),
                   pl.BlockSpec((D_FF_PAD, wblk), by_cols), pl.BlockSpec((wblk, d), by_rows)],
        out_shape=[jax.ShapeDtypeStruct((batch, seq, FNET_WIDTH), BF16),
                   jax.ShapeDtypeStruct((d, D_FF_PAD), BF16), jax.ShapeDtypeStruct((d, D_FF_PAD), BF16),
                   jax.ShapeDtypeStruct((D_FF_PAD, d), BF16), jax.ShapeDtypeStruct((d, d), BF16)],
        scratch_shapes=[pltpu.VMEM((FOURIER_BATCH, half, FNET_WIDTH), BF16),
                        pltpu.VMEM((FOURIER_BATCH, half, FNET_WIDTH), BF16)],
        compiler_params=params,
        name="fourier",
    )(uf.reshape(batch, seq, FNET_WIDTH), jnp.asarray(cos_s).astype(BF16), jnp.asarray(sin_s).astype(BF16), rev,
      cos_c * ortho, sin_c * ortho, fnet_w.astype(F32), fnet_b.reshape(1, FNET_WIDTH).astype(F32),
      ffn2_w_gate.T, ffn2_w_up.T, ffn2_w_down, w_out)

    a_out = pl.pallas_call(
        _attention_kernel,
        grid=(batch,),
        in_specs=[pl.BlockSpec((N_KV_HEADS, seq, 2 * LANES), lambda b: (0, b, 0)),
                  pl.BlockSpec((2 * N_KV_HEADS, seq, LANES), lambda b: (0, b, 0)),
                  pl.BlockSpec((None, N_KV_HEADS, HEAD_DIM, seq), lambda b: (b, 0, 0, 0))],
        out_specs=pl.BlockSpec((N_KV_HEADS, seq, 2 * LANES), lambda b: (0, b, 0)),
        out_shape=jax.ShapeDtypeStruct((N_KV_HEADS, n_tok, 2 * LANES), BF16),
        scratch_shapes=[pltpu.VMEM((2 * N_KV_HEADS, seq, 2 * ATTN_Q_TILE), F32),
                        pltpu.VMEM((2 * N_KV_HEADS, seq, 2 * ATTN_Q_TILE), BF16),
                        pltpu.VMEM((2 * N_KV_HEADS, 2, 1, 2 * ATTN_Q_TILE), F32),
                        pltpu.VMEM((2 * N_KV_HEADS, HEAD_DIM, 2 * ATTN_Q_TILE), F32)],
        compiler_params=params,
        name="attention",
    )(q, kp, vt)

    y = pl.pallas_call(
        _outproj_ffn2_kernel,
        grid=(n_tok // TOKEN_TILE,),
        in_specs=[tok(d), tok(FNET_WIDTH), grp(N_KV_HEADS, 2 * LANES),
                  _resident((FNET_WIDTH, d)), _resident((N_KV_HEADS, 2 * LANES, d)), _resident((1, d)),
                  _resident((d, D_FF_PAD)), _resident((d, D_FF_PAD)), _resident((D_FF_PAD, d)),
                  _resident((1, d))],
        out_specs=tok(d),
        out_shape=jax.ShapeDtypeStruct((n_tok, d), F32),
        compiler_params=params,
        name="outproj_ffn2",
    )(x1, f_out.reshape(n_tok, FNET_WIDTH), a_out,
      w_out_bf[:FNET_WIDTH], w_out_bf[FNET_WIDTH:].reshape(N_KV_HEADS, 2 * LANES, d), _row(ffn2_norm),
      wg2, wu2, wd2, _row(final_norm))
    return y.reshape(batch, seq, d)
```

```python
import math

import numpy as np
import jax
import jax.numpy as jnp
from jax import lax
from jax.experimental import pallas as pl
from jax.experimental.pallas import tpu as pltpu

D_MODEL = 1024
D_FF = 2752
FNET_WIDTH = 512
FNET_GROUPS = 4
FNET_GROUP_DIM = 128
HEAD_DIM = 64
N_Q_HEADS = 8
N_KV_HEADS = 2
ATTN_WIDTH = N_Q_HEADS * HEAD_DIM
KV_WIDTH = N_KV_HEADS * HEAD_DIM
QK_WIDTH = ATTN_WIDTH + KV_WIDTH
IN_WIDTH = FNET_WIDTH + ATTN_WIDTH + 2 * KV_WIDTH
GRID_W = 64
AXIS_DIM = HEAD_DIM // 2
ROPE_THETA = 10000.0
EPS = 1e-6

LANES = 128
BF16_SUBLANES = 16
MXU_DIM = 256
VMEM_LIMIT_BYTES = 60000 * 1024

D_FF_PAD = ((D_FF + MXU_DIM - 1) // MXU_DIM) * MXU_DIM

TOKEN_TILE = 1024
SUB_TILE = 256
ATTN_Q_TILE = 256
WEIGHT_CAST_STEPS = 8
FOURIER_BATCH = 2

BF16 = jnp.bfloat16
F32 = jnp.float32


def _rms_norm(x, gain):
    return x * lax.rsqrt(jnp.mean(x * x, axis=-1, keepdims=True) + EPS) * gain


def _dot(a, b):
    return jnp.dot(a, b, preferred_element_type=F32)


def _swiglu_half_step(x, gain, wg_ref, wu_ref, wd_ref):
    xn = _rms_norm(x, gain).astype(BF16)
    g = _dot(xn, wg_ref[...])
    u = _dot(xn, wu_ref[...])
    a = (g * jax.nn.sigmoid(g) * u).astype(BF16)
    return x + 0.5 * _dot(a, wd_ref[...])


def _split_hi_lo(x):
    hi = x.astype(BF16)
    lo = (x - hi.astype(F32)).astype(BF16)
    return hi, lo


def _ffn1_inproj_kernel(x_ref, g1_ref, wg_ref, wu_ref, wd_ref, gm_ref, win_ref, gqk_ref,
                        cos_ref, sin_ref, seg_ref,
                        x1_ref, uf_ref, q_ref, kp_ref, vt_ref):
    seg = seg_ref[...]
    lane = lax.broadcasted_iota(jnp.int32, (1, LANES), 1)
    upper_half = (lane & (AXIS_DIM // 2)) != 0

    def gate_up(rows):
        xn = _rms_norm(x_ref[rows, :], g1_ref[...]).astype(BF16)
        g = _dot(xn, wg_ref[...])
        return (g * jax.nn.sigmoid(g) * _dot(xn, wu_ref[...])).astype(BF16)

    def down(rows, act):
        x1_ref[rows, :] = x_ref[rows, :] + 0.5 * _dot(act, wd_ref[...])

    def in_proj(rows):
        h = _rms_norm(x1_ref[rows, :], gm_ref[...]).astype(BF16)
        return _dot(h, win_ref[...])

    def heads(rows, u):
        uf_ref[rows, :] = u[:, :FNET_WIDTH].astype(BF16)
        vt = u[:, FNET_WIDTH + QK_WIDTH:].T.astype(BF16)
        for g in range(N_KV_HEADS):
            vt_ref[g, :, rows] = vt[g * HEAD_DIM:(g + 1) * HEAD_DIM]

        cos_t = cos_ref[rows, :]
        sin_t = sin_ref[rows, :]
        for c in range(QK_WIDTH // LANES):
            lo_col = FNET_WIDTH + c * LANES
            z = u[:, lo_col:lo_col + LANES]
            hi, lo = _split_hi_lo(z * z)
            mean_sq = (_dot(hi, seg) + _dot(lo, seg)) * (1.0 / HEAD_DIM)
            zn = z * lax.rsqrt(mean_sq + EPS) * gqk_ref[:, c * LANES:(c + 1) * LANES]
            partner = jnp.where(upper_half,
                                pltpu.roll(zn, AXIS_DIM // 2, axis=1),
                                pltpu.roll(zn, LANES - AXIS_DIM // 2, axis=1))
            out = zn * cos_t + partner * sin_t
            if c < ATTN_WIDTH // LANES:
                q_ref[c // 2, rows, (c % 2) * LANES:(c % 2 + 1) * LANES] = out.astype(BF16)
            else:
                swapped = pltpu.roll(out, HEAD_DIM, axis=1)
                first = lane < HEAD_DIM
                variants = (jnp.where(first, out, 0.0), jnp.where(first, 0.0, swapped),
                            jnp.where(first, swapped, 0.0), jnp.where(first, 0.0, out))
                for i, kv in enumerate(variants):
                    kp_ref[i, rows, :] = kv.astype(BF16)

    sub_tiles = [slice(r * SUB_TILE, (r + 1) * SUB_TILE) for r in range(TOKEN_TILE // SUB_TILE)]
    for slot in range(len(sub_tiles) + 1):
        cur = sub_tiles[slot] if slot < len(sub_tiles) else None
        prev = sub_tiles[slot - 1] if slot > 0 else None
        if cur is not None:
            act = gate_up(cur)
        if prev is not None:
            u_prev = in_proj(prev)
        if cur is not None:
            down(cur, act)
        if prev is not None:
            heads(prev, u_prev)


def _fourier_kernel(uf_ref, ch_ref, sh_ref, rev_ref, cc_ref, sc_ref, fw_ref, fb_ref,
                    wgt_ref, wut_ref, wd_ref, wo_ref,
                    out_ref, wg_out_ref, wu_out_ref, wd_out_ref, wo_out_ref, e1_ref, o2_ref):
    n_batch, n = uf_ref.shape[0], uf_ref.shape[1]
    h = n // 2
    rev = rev_ref[...]

    cc_hi, cc_lo = _split_hi_lo(cc_ref[...])
    sc_hi, sc_lo = _split_hi_lo(sc_ref[...])
    folded = []
    for g in range(FNET_GROUPS):
        w_hi, w_lo = _split_hi_lo(fw_ref[g])
        folded.append(((_dot(cc_hi, w_hi) + _dot(cc_hi, w_lo) + _dot(cc_lo, w_hi)).astype(BF16),
                       (_dot(sc_hi, w_hi) + _dot(sc_hi, w_lo) + _dot(sc_lo, w_hi)).astype(BF16)))

    def mirror(b):
        mirrored = _dot(rev, uf_ref[b, h:, :])
        lo32 = uf_ref[b, :h, :].astype(F32)
        return (lo32 + mirrored).astype(BF16), (lo32 - mirrored).astype(BF16)

    def channel_mix(b, even, odd):
        mid_rows = []
        for g, (wc, ws) in enumerate(folded):
            cols = slice(g * FNET_GROUP_DIM, (g + 1) * FNET_GROUP_DIM)
            e1_ref[b, :, cols] = _dot(even[:, cols], wc).astype(BF16)
            o2_ref[b, :, cols] = _dot(odd[:, cols], ws).astype(BF16)
            mid_rows.append(_dot(uf_ref[b, h:h + BF16_SUBLANES, cols], wc)[0:1])
        return jnp.concatenate(mid_rows, axis=-1)

    def position_dft(b, mid):
        a = _dot(ch_ref[...], e1_ref[b])
        s = _dot(sh_ref[...], o2_ref[b])
        j = lax.broadcasted_iota(jnp.int32, (a.shape[0], 1), 0)
        a = a + jnp.where((j & 1) == 0, 1.0, -1.0) * mid + fb_ref[...]
        out_ref[b, :h, :] = (a[:h] - s[:h]).astype(BF16)
        return (a + s).astype(BF16)

    def mirror_out(b, z):
        out_ref[b, h:, :] = _dot(rev, z[:h]).astype(BF16)
        out_ref[b, h:h + 1, :] = z[h:h + 1]

    halves = [mirror(b) for b in range(n_batch)]
    mids = [channel_mix(b, *halves[b]) for b in range(n_batch)]
    zs = [position_dft(b, mids[b]) for b in range(n_batch)]
    for b in range(n_batch):
        mirror_out(b, zs[b])

    _cast_pad_cols_kernel(wgt_ref, wut_ref, wg_out_ref, wu_out_ref)
    _cast_pad_rows_kernel(wd_ref, wd_out_ref)
    _cast_kernel(wo_ref, wo_out_ref)


def _attention_kernel(q_ref, kp_ref, vt_ref, o_ref, s_ref, p_ref, m_ref, att_ref):
    seq = kp_ref.shape[1]
    tq = ATTN_Q_TILE
    n_tiles = seq // tq
    n_streams = 2 * N_KV_HEADS
    ones_rows = jnp.ones((BF16_SUBLANES, seq), BF16)

    def q_rows(t):
        start = t * tq
        if not isinstance(start, int):
            start = pl.multiple_of(start, tq)
        return pl.ds(start, tq)

    def qk(t, r):
        g = r // 2
        qblk = q_ref[g, q_rows(t), :]
        qg = jnp.concatenate([qblk[:, :LANES], qblk[:, LANES:]], axis=0)
        s = lax.dot_general(kp_ref[r], qg, (((1,), (1,)), ((), ())), preferred_element_type=F32)
        s_ref[r] = s
        m_ref[r, t % 2] = jnp.max(s, axis=0, keepdims=True)

    def ex(t, r):
        p_ref[r] = jnp.exp2(s_ref[r] - m_ref[r, t % 2]).astype(BF16)

    def pv(r):
        v1t = jnp.concatenate([vt_ref[r // 2], ones_rows], axis=0)
        o = _dot(v1t, p_ref[r])
        att_ref[r] = o[:HEAD_DIM] * (1.0 / o[HEAD_DIM:HEAD_DIM + 1])

    def emit(t, g):
        for c in range(2):
            pair = jnp.concatenate([att_ref[2 * g + h, :, c * tq:(c + 1) * tq] for h in range(2)], axis=0)
            o_ref[g, q_rows(t), c * LANES:(c + 1) * LANES] = pair.T.astype(BF16)

    def tile(t, first=False, last=False):
        for r in range(n_streams):
            ex(t, r)
            if not last:
                qk(t + 1, r)
            if r > 0:
                pv(r - 1)
                if r % 2 == 0:
                    emit(t, r // 2 - 1)
            elif not first:
                pv(n_streams - 1)
                emit(t - 1, N_KV_HEADS - 1)

    def loop_body(t, carry):
        tile(t)
        return carry

    for r in range(n_streams):
        qk(0, r)
    tile(0, first=True)
    lax.fori_loop(1, n_tiles - 1, loop_body, 0)
    tile(n_tiles - 1, last=True)
    pv(n_streams - 1)
    emit(n_tiles - 1, N_KV_HEADS - 1)


def _outproj_ffn2_kernel(x1_ref, f_ref, a_ref, wof_ref, woa_ref, g2_ref, wg_ref, wu_ref, wd_ref,
                         gf_ref, y_ref):
    sub_tiles = [slice(r * SUB_TILE, (r + 1) * SUB_TILE) for r in range(TOKEN_TILE // SUB_TILE)]
    for rows in sub_tiles:
        x2 = x1_ref[rows, :] + _dot(f_ref[rows, :], wof_ref[...])
        for g in range(N_KV_HEADS):
            x2 = x2 + _dot(a_ref[g, rows, :], woa_ref[g])
        y_ref[rows, :] = x2
    for rows in sub_tiles:
        x3 = _swiglu_half_step(y_ref[rows, :], g2_ref[...], wg_ref, wu_ref, wd_ref)
        y_ref[rows, :] = _rms_norm(x3, gf_ref[...])


def _resident(shape):
    return pl.BlockSpec(shape, lambda *_: (0,) * len(shape), pipeline_mode=pl.Buffered(1))


def _rope_tables(seq):
    rows = seq // GRID_W
    row_idx = np.repeat(np.arange(rows, dtype=np.float64), GRID_W)
    col_idx = np.tile(np.arange(GRID_W, dtype=np.float64), rows)
    inv_freq = ROPE_THETA ** (-np.arange(0, AXIS_DIM, 2, dtype=np.float64) / AXIS_DIM)
    ang_r = row_idx[:, None] * inv_freq[None, :]
    ang_c = col_idx[:, None] * inv_freq[None, :]
    cos_h = np.concatenate([np.cos(ang_r)] * 2 + [np.cos(ang_c)] * 2, axis=-1)
    sin_h = np.concatenate([-np.sin(ang_r), np.sin(ang_r), -np.sin(ang_c), np.sin(ang_c)], axis=-1)
    reps = LANES // HEAD_DIM
    return (np.tile(cos_h, (1, reps)).astype(np.float32), np.tile(sin_h, (1, reps)).astype(np.float32))


def _dft_cos_sin(n, rows, cols):
    j = np.arange(rows, dtype=np.int64)
    k = np.arange(cols, dtype=np.int64)
    ang = ((j[:, None] * k[None, :]) % n).astype(np.float64) * (2.0 * math.pi / n)
    return np.cos(ang).astype(np.float32), np.sin(ang).astype(np.float32)


def _cast_pad_cols_kernel(*refs):
    n = len(refs) // 2
    aligned = (D_FF // LANES) * LANES
    for wt_ref, o_ref in zip(refs[:n], refs[n:]):
        o_ref[:, :aligned] = wt_ref[:aligned, :].T.astype(BF16)
        tail = jnp.concatenate([wt_ref[aligned:, :], jnp.zeros((D_FF_PAD - D_FF, wt_ref.shape[1]), F32)], axis=0)
        o_ref[:, aligned:] = tail.T.astype(BF16)


def _cast_pad_rows_kernel(*refs):
    n = len(refs) // 2
    for w_ref, o_ref in zip(refs[:n], refs[n:]):
        o_ref[:D_FF, :] = w_ref[...].astype(BF16)
        o_ref[D_FF:, :] = jnp.zeros((D_FF_PAD - D_FF, o_ref.shape[1]), BF16)


def _cast_kernel(*refs):
    n = len(refs) // 2
    for w_ref, o_ref in zip(refs[:n], refs[n:]):
        o_ref[...] = w_ref[...].astype(BF16)


def _cast_ffn1_kernel(wgt_ref, wut_ref, wd_ref, win_ref, wg_out_ref, wu_out_ref, wd_out_ref, win_out_ref):
    _cast_pad_cols_kernel(wgt_ref, wut_ref, wg_out_ref, wu_out_ref)
    _cast_pad_rows_kernel(wd_ref, wd_out_ref)
    _cast_kernel(win_ref, win_out_ref)


def _row(v):
    return v.reshape(1, -1).astype(F32)


def kernel(x, ffn1_norm, ffn1_w_gate, ffn1_w_up, ffn1_w_down, mix_norm, w_in, fnet_w, fnet_b, q_norm,
           k_norm, w_out, ffn2_norm, ffn2_w_gate, ffn2_w_up, ffn2_w_down, final_norm):
    batch, seq, d = x.shape
    assert d == D_MODEL and seq % TOKEN_TILE == 0 and seq % ATTN_Q_TILE == 0 and seq % GRID_W == 0
    n_tok = batch * seq
    tiles_per_seq = seq // TOKEN_TILE
    x2d = x.reshape(n_tok, d)

    params = pltpu.CompilerParams(dimension_semantics=("arbitrary",), vmem_limit_bytes=VMEM_LIMIT_BYTES)
    tok = lambda width: pl.BlockSpec((TOKEN_TILE, width), lambda i: (i, 0))

    blk = d // WEIGHT_CAST_STEPS
    by_rows, by_cols = (lambda i: (i, 0)), (lambda i: (0, i))
    wg1, wu1, wd1, w_in_bf = pl.pallas_call(
        _cast_ffn1_kernel,
        grid=(WEIGHT_CAST_STEPS,),
        in_specs=[pl.BlockSpec((D_FF, blk), by_cols), pl.BlockSpec((D_FF, blk), by_cols),
                  pl.BlockSpec((D_FF, blk), by_cols), pl.BlockSpec((blk, IN_WIDTH), by_rows)],
        out_specs=[pl.BlockSpec((blk, D_FF_PAD), by_rows), pl.BlockSpec((blk, D_FF_PAD), by_rows),
                   pl.BlockSpec((D_FF_PAD, blk), by_cols), pl.BlockSpec((blk, IN_WIDTH), by_rows)],
        out_shape=[jax.ShapeDtypeStruct((d, D_FF_PAD), BF16), jax.ShapeDtypeStruct((d, D_FF_PAD), BF16),
                   jax.ShapeDtypeStruct((D_FF_PAD, d), BF16), jax.ShapeDtypeStruct((d, IN_WIDTH), BF16)],
        compiler_params=params,
        name="cast_ffn1",
    )(ffn1_w_gate.T, ffn1_w_up.T, ffn1_w_down, w_in)

    cos_t, sin_t = _rope_tables(seq)
    scale = math.log2(math.e) / math.sqrt(HEAD_DIM)
    gain_qk = jnp.concatenate([jnp.tile(q_norm.astype(F32) * scale, N_Q_HEADS),
                               jnp.tile(k_norm.astype(F32), N_KV_HEADS)]).reshape(1, QK_WIDTH)
    head_of_lane = np.arange(LANES) // HEAD_DIM
    seg_ones = jnp.asarray(head_of_lane[:, None] == head_of_lane[None, :], dtype=BF16)
    rope_spec = pl.BlockSpec((TOKEN_TILE, LANES), lambda i: (i % tiles_per_seq, 0))

    grp = lambda n, width: pl.BlockSpec((n, TOKEN_TILE, width), lambda i: (0, i, 0))
    vt_spec = pl.BlockSpec((None, N_KV_HEADS, HEAD_DIM, TOKEN_TILE),
                           lambda i: (i // tiles_per_seq, 0, 0, i % tiles_per_seq))
    x1, uf, q, kp, vt = pl.pallas_call(
        _ffn1_inproj_kernel,
        grid=(n_tok // TOKEN_TILE,),
        in_specs=[tok(d), _resident((1, d)),
                  _resident((d, D_FF_PAD)), _resident((d, D_FF_PAD)), _resident((D_FF_PAD, d)),
                  _resident((1, d)), _resident((d, IN_WIDTH)), _resident((1, QK_WIDTH)),
                  rope_spec, rope_spec, _resident((LANES, LANES))],
        out_specs=[tok(d), tok(FNET_WIDTH), grp(N_KV_HEADS, 2 * LANES), grp(2 * N_KV_HEADS, LANES), vt_spec],
        out_shape=[jax.ShapeDtypeStruct((n_tok, d), F32),
                   jax.ShapeDtypeStruct((n_tok, FNET_WIDTH), BF16),
                   jax.ShapeDtypeStruct((N_KV_HEADS, n_tok, 2 * LANES), BF16),
                   jax.ShapeDtypeStruct((2 * N_KV_HEADS, n_tok, LANES), BF16),
                   jax.ShapeDtypeStruct((batch, N_KV_HEADS, HEAD_DIM, seq), BF16)],
        compiler_params=params,
        name="ffn1_inproj",
    )(x2d, _row(ffn1_norm), wg1, wu1, wd1, _row(mix_norm), w_in_bf, gain_qk, cos_t, sin_t, seg_ones)

    half = seq // 2
    dft_rows = half + BF16_SUBLANES
    cos_s, sin_s = _dft_cos_sin(seq, dft_rows, half)
    idx = np.arange(half)
    rev = jnp.asarray((idx[:, None] + idx[None, :]) == half, dtype=BF16)
    cos_c, sin_c = _dft_cos_sin(FNET_GROUP_DIM, FNET_GROUP_DIM, FNET_GROUP_DIM)
    ortho = 1.0 / math.sqrt(seq * FNET_GROUP_DIM)
    fourier_steps = batch // FOURIER_BATCH
    wblk = d // fourier_steps
    assert d % fourier_steps == 0 and wblk % LANES == 0
    f_out, wg2, wu2, wd2, w_out_bf = pl.pallas_call(
        _fourier_kernel,
        grid=(fourier_steps,),
        in_specs=[pl.BlockSpec((FOURIER_BATCH, seq, FNET_WIDTH), lambda b: (b, 0, 0)),
                  _resident((dft_rows, half)), _resident((dft_rows, half)), _resident((half, half)),
                  _resident((FNET_GROUP_DIM, FNET_GROUP_DIM)), _resident((FNET_GROUP_DIM, FNET_GROUP_DIM)),
                  _resident((FNET_GROUPS, FNET_GROUP_DIM, FNET_GROUP_DIM)), _resident((1, FNET_WIDTH)),
                  pl.BlockSpec((D_FF, wblk), by_cols), pl.BlockSpec((D_FF, wblk), by_cols),
                  pl.BlockSpec((D_FF, wblk), by_cols), pl.BlockSpec((wblk, d), by_rows)],
        out_specs=[pl.BlockSpec((FOURIER_BATCH, seq, FNET_WIDTH), lambda b: (b, 0, 0)),
                   pl.BlockSpec((wblk, D_FF_PAD), by_rows), pl.BlockSpec((wblk, D_FF_PAD), by_rows),
                   pl.BlockSpec((D_FF_PAD, wblk), by_cols), pl.BlockSpec((wblk, d), by_rows)],
        out_shape=[jax.ShapeDtypeStruct((batch, seq, FNET_WIDTH), BF16),
                   jax.ShapeDtypeStruct((d, D_FF_PAD), BF16), jax.ShapeDtypeStruct((d, D_FF_PAD), BF16),
                   jax.ShapeDtypeStruct((D_FF_PAD, d), BF16), jax.ShapeDtypeStruct((d, d), BF16)],
        scratch_shapes=[pltpu.VMEM((FOURIER_BATCH, half, FNET_WIDTH), BF16),
                        pltpu.VMEM((FOURIER_BATCH, half, FNET_WIDTH), BF16)],
        compiler_params=params,
        name="fourier",
    )(uf.reshape(batch, seq, FNET_WIDTH), jnp.asarray(cos_s).astype(BF16), jnp.asarray(sin_s).astype(BF16), rev,
      cos_c * ortho, sin_c * ortho, fnet_w.astype(F32), fnet_b.reshape(1, FNET_WIDTH).astype(F32),
      ffn2_w_gate.T, ffn2_w_up.T, ffn2_w_down, w_out)

    a_out = pl.pallas_call(
        _attention_kernel,
        grid=(batch,),
        in_specs=[pl.BlockSpec((N_KV_HEADS, seq, 2 * LANES), lambda b: (0, b, 0)),
                  pl.BlockSpec((2 * N_KV_HEADS, seq, LANES), lambda b: (0, b, 0)),
                  pl.BlockSpec((None, N_KV_HEADS, HEAD_DIM, seq), lambda b: (b, 0, 0, 0))],
        out_specs=pl.BlockSpec((N_KV_HEADS, seq, 2 * LANES), lambda b: (0, b, 0)),
        out_shape=jax.ShapeDtypeStruct((N_KV_HEADS, n_tok, 2 * LANES), BF16),
        scratch_shapes=[pltpu.VMEM((2 * N_KV_HEADS, seq, 2 * ATTN_Q_TILE), F32),
                        pltpu.VMEM((2 * N_KV_HEADS, seq, 2 * ATTN_Q_TILE), BF16),
                        pltpu.VMEM((2 * N_KV_HEADS, 2, 1, 2 * ATTN_Q_TILE), F32),
                        pltpu.VMEM((2 * N_KV_HEADS, HEAD_DIM, 2 * ATTN_Q_TILE), F32)],
        compiler_params=params,
        name="attention",
    )(q, kp, vt)

    y = pl.pallas_call(
        _outproj_ffn2_kernel,
        grid=(n_tok // TOKEN_TILE,),
        in_specs=[tok(d), tok(FNET_WIDTH), grp(N_KV_HEADS, 2 * LANES),
                  _resident((FNET_WIDTH, d)), _resident((N_KV_HEADS, 2 * LANES, d)), _resident((1, d)),
                  _resident((d, D_FF_PAD)), _resident((d, D_FF_PAD)), _resident((D_FF_PAD, d)),
                  _resident((1, d))],
        out_specs=tok(d),
        out_shape=jax.ShapeDtypeStruct((n_tok, d), F32),
        compiler_params=params,
        name="outproj_ffn2",
    )(x1, f_out.reshape(n_tok, FNET_WIDTH), a_out,
      w_out_bf[:FNET_WIDTH], w_out_bf[FNET_WIDTH:].reshape(N_KV_HEADS, 2 * LANES, d), _row(ffn2_norm),
      wg2, wu2, wd2, _row(final_norm))
    return y.reshape(batch, seq, d)
```

```python
import math

import numpy as np
import jax
import jax.numpy as jnp
from jax import lax
from jax.experimental import pallas as pl
from jax.experimental.pallas import tpu as pltpu

D_MODEL = 1024
D_FF = 2752
FNET_WIDTH = 512
FNET_GROUPS = 4
FNET_GROUP_DIM = 128
HEAD_DIM = 64
N_Q_HEADS = 8
N_KV_HEADS = 2
ATTN_WIDTH = N_Q_HEADS * HEAD_DIM
KV_WIDTH = N_KV_HEADS * HEAD_DIM
QK_WIDTH = ATTN_WIDTH + KV_WIDTH
IN_WIDTH = FNET_WIDTH + ATTN_WIDTH + 2 * KV_WIDTH
GRID_W = 64
AXIS_DIM = HEAD_DIM // 2
ROPE_THETA = 10000.0
EPS = 1e-6

LANES = 128
BF16_SUBLANES = 16
MXU_DIM = 256
VMEM_LIMIT_BYTES = 60000 * 1024

D_FF_PAD = ((D_FF + MXU_DIM - 1) // MXU_DIM) * MXU_DIM

TOKEN_TILE = 1024
SUB_TILE = 256
ATTN_Q_TILE = 256
WEIGHT_CAST_STEPS = 8
FOURIER_BATCH = 2

BF16 = jnp.bfloat16
F32 = jnp.float32


def _rms_norm(x, gain):
    return x * lax.rsqrt(jnp.mean(x * x, axis=-1, keepdims=True) + EPS) * gain


def _dot(a, b):
    return jnp.dot(a, b, preferred_element_type=F32)


def _swiglu_half_step(x, gain, wg_ref, wu_ref, wd_ref):
    xn = _rms_norm(x, gain).astype(BF16)
    g = _dot(xn, wg_ref[...])
    u = _dot(xn, wu_ref[...])
    a = (g * jax.nn.sigmoid(g) * u).astype(BF16)
    return x + 0.5 * _dot(a, wd_ref[...])


def _split_hi_lo(x):
    hi = x.astype(BF16)
    lo = (x - hi.astype(F32)).astype(BF16)
    return hi, lo


def _ffn1_inproj_kernel(x_ref, g1_ref, wg_ref, wu_ref, wd_ref, gm_ref, win_ref, gqk_ref,
                        cos_ref, sin_ref, seg_ref,
                        x1_ref, uf_ref, q_ref, kp_ref, vt_ref):
    seg = seg_ref[...]
    lane = lax.broadcasted_iota(jnp.int32, (1, LANES), 1)
    upper_half = (lane & (AXIS_DIM // 2)) != 0

    def gate_up(rows):
        xn = _rms_norm(x_ref[rows, :], g1_ref[...]).astype(BF16)
        g = _dot(xn, wg_ref[...])
        return (g * jax.nn.sigmoid(g) * _dot(xn, wu_ref[...])).astype(BF16)

    def down(rows, act):
        x1_ref[rows, :] = x_ref[rows, :] + 0.5 * _dot(act, wd_ref[...])

    def in_proj(rows):
        h = _rms_norm(x1_ref[rows, :], gm_ref[...]).astype(BF16)
        return _dot(h, win_ref[...])

    def heads(rows, u):
        uf_ref[rows, :] = u[:, :FNET_WIDTH].astype(BF16)
        vt = u[:, FNET_WIDTH + QK_WIDTH:].T.astype(BF16)
        for g in range(N_KV_HEADS):
            vt_ref[g, :, rows] = vt[g * HEAD_DIM:(g + 1) * HEAD_DIM]

        cos_t = cos_ref[rows, :]
        sin_t = sin_ref[rows, :]
        for c in range(QK_WIDTH // LANES):
            lo_col = FNET_WIDTH + c * LANES
            z = u[:, lo_col:lo_col + LANES]
            hi, lo = _split_hi_lo(z * z)
            mean_sq = (_dot(hi, seg) + _dot(lo, seg)) * (1.0 / HEAD_DIM)
            zn = z * lax.rsqrt(mean_sq + EPS) * gqk_ref[:, c * LANES:(c + 1) * LANES]
            partner = jnp.where(upper_half,
                                pltpu.roll(zn, AXIS_DIM // 2, axis=1),
                                pltpu.roll(zn, LANES - AXIS_DIM // 2, axis=1))
            out = zn * cos_t + partner * sin_t
            if c < ATTN_WIDTH // LANES:
                q_ref[c // 2, rows, (c % 2) * LANES:(c % 2 + 1) * LANES] = out.astype(BF16)
            else:
                swapped = pltpu.roll(out, HEAD_DIM, axis=1)
                first = lane < HEAD_DIM
                variants = (jnp.where(first, out, 0.0), jnp.where(first, 0.0, swapped),
                            jnp.where(first, swapped, 0.0), jnp.where(first, 0.0, out))
                for i, kv in enumerate(variants):
                    kp_ref[i, rows, :] = kv.astype(BF16)

    sub_tiles = [slice(r * SUB_TILE, (r + 1) * SUB_TILE) for r in range(TOKEN_TILE // SUB_TILE)]
    for slot in range(len(sub_tiles) + 1):
        cur = sub_tiles[slot] if slot < len(sub_tiles) else None
        prev = sub_tiles[slot - 1] if slot > 0 else None
        if cur is not None:
            act = gate_up(cur)
        if prev is not None:
            u_prev = in_proj(prev)
        if cur is not None:
            down(cur, act)
        if prev is not None:
            heads(prev, u_prev)


def _fourier_kernel(uf_ref, ch_ref, sh_ref, rev_ref, cc_ref, sc_ref, fw_ref, fb_ref,
                    wgt_ref, wut_ref, wd_ref, wo_ref,
                    out_ref, wg_out_ref, wu_out_ref, wd_out_ref, wo_out_ref, e1_ref, o2_ref):
    n_batch, n = uf_ref.shape[0], uf_ref.shape[1]
    h = n // 2
    rev = rev_ref[...]

    cc_hi, cc_lo = _split_hi_lo(cc_ref[...])
    sc_hi, sc_lo = _split_hi_lo(sc_ref[...])
    folded = []
    for g in range(FNET_GROUPS):
        w_hi, w_lo = _split_hi_lo(fw_ref[g])
        folded.append(((_dot(cc_hi, w_hi) + _dot(cc_hi, w_lo) + _dot(cc_lo, w_hi)).astype(BF16),
                       (_dot(sc_hi, w_hi) + _dot(sc_hi, w_lo) + _dot(sc_lo, w_hi)).astype(BF16)))

    def mirror(b):
        mirrored = _dot(rev, uf_ref[b, h:, :])
        lo32 = uf_ref[b, :h, :].astype(F32)
        return (lo32 + mirrored).astype(BF16), (lo32 - mirrored).astype(BF16)

    def channel_mix(b, even, odd):
        mid_rows = []
        for g, (wc, ws) in enumerate(folded):
            cols = slice(g * FNET_GROUP_DIM, (g + 1) * FNET_GROUP_DIM)
            e1_ref[b, :, cols] = _dot(even[:, cols], wc).astype(BF16)
            o2_ref[b, :, cols] = _dot(odd[:, cols], ws).astype(BF16)
            mid_rows.append(_dot(uf_ref[b, h:h + BF16_SUBLANES, cols], wc)[0:1])
        return jnp.concatenate(mid_rows, axis=-1)

    def position_dft(b, mid):
        a = _dot(ch_ref[...], e1_ref[b])
        s = _dot(sh_ref[...], o2_ref[b])
        j = lax.broadcasted_iota(jnp.int32, (a.shape[0], 1), 0)
        a = a + jnp.where((j & 1) == 0, 1.0, -1.0) * mid + fb_ref[...]
        out_ref[b, :h, :] = (a[:h] - s[:h]).astype(BF16)
        return (a + s).astype(BF16)

    def mirror_out(b, z):
        out_ref[b, h:, :] = _dot(rev, z[:h]).astype(BF16)
        out_ref[b, h:h + 1, :] = z[h:h + 1]

    halves = [mirror(b) for b in range(n_batch)]
    mids = [channel_mix(b, *halves[b]) for b in range(n_batch)]
    zs = [position_dft(b, mids[b]) for b in range(n_batch)]
    for b in range(n_batch):
        mirror_out(b, zs[b])

    _cast_pad_cols_kernel(wgt_ref, wut_ref, wg_out_ref, wu_out_ref)
    _cast_pad_rows_kernel(wd_ref, wd_out_ref)
    _cast_kernel(wo_ref, wo_out_ref)


def _attention_kernel(q_ref, kp_ref, vt_ref, o_ref, s_ref, p_ref, m_ref, att_ref):
    seq = kp_ref.shape[1]
    tq = ATTN_Q_TILE
    n_tiles = seq // tq
    n_streams = 2 * N_KV_HEADS
    ones_rows = jnp.ones((BF16_SUBLANES, seq), BF16)

    def q_rows(t):
        start = t * tq
        if not isinstance(start, int):
            start = pl.multiple_of(start, tq)
        return pl.ds(start, tq)

    def qk(t, r):
        g = r // 2
        qblk = q_ref[g, q_rows(t), :]
        qg = jnp.concatenate([qblk[:, :LANES], qblk[:, LANES:]], axis=0)
        s = lax.dot_general(kp_ref[r], qg, (((1,), (1,)), ((), ())), preferred_element_type=F32)
        s_ref[r] = s
        m_ref[r, t % 2] = jnp.max(s, axis=0, keepdims=True)

    def ex(t, r):
        p_ref[r] = jnp.exp2(s_ref[r] - m_ref[r, t % 2]).astype(BF16)

    def pv(r):
        v1t = jnp.concatenate([vt_ref[r // 2], ones_rows], axis=0)
        o = _dot(v1t, p_ref[r])
        att_ref[r] = o[:HEAD_DIM] * (1.0 / o[HEAD_DIM:HEAD_DIM + 1])

    def emit(t, g):
        for c in range(2):
            pair = jnp.concatenate([att_ref[2 * g + h, :, c * tq:(c + 1) * tq] for h in range(2)], axis=0)
            o_ref[g, q_rows(t), c * LANES:(c + 1) * LANES] = pair.T.astype(BF16)

    def tile(t, first=False, last=False):
        for r in range(n_streams):
            ex(t, r)
            if not last:
                qk(t + 1, r)
            if r > 0:
                pv(r - 1)
                if r % 2 == 0:
                    emit(t, r // 2 - 1)
            elif not first:
                pv(n_streams - 1)
                emit(t - 1, N_KV_HEADS - 1)

    def loop_body(t, carry):
        tile(t)
        return carry

    for r in range(n_streams):
        qk(0, r)
    tile(0, first=True)
    lax.fori_loop(1, n_tiles - 1, loop_body, 0)
    tile(n_tiles - 1, last=True)
    pv(n_streams - 1)
    emit(n_tiles - 1, N_KV_HEADS - 1)


def _outproj_ffn2_kernel(x1_ref, f_ref, a_ref, wo_ref, g2_ref, wg_ref, wu_ref, wd_ref, gf_ref, y_ref):
    sub_tiles = [slice(r * SUB_TILE, (r + 1) * SUB_TILE) for r in range(TOKEN_TILE // SUB_TILE)]
    for rows in sub_tiles:
        x2 = x1_ref[rows, :] + _dot(f_ref[rows, :], wo_ref[:FNET_WIDTH, :])
        for g in range(N_KV_HEADS):
            w_rows = slice(FNET_WIDTH + g * 2 * LANES, FNET_WIDTH + (g + 1) * 2 * LANES)
            x2 = x2 + _dot(a_ref[g, rows, :], wo_ref[w_rows, :])
        y_ref[rows, :] = x2
    for rows in sub_tiles:
        x3 = _swiglu_half_step(y_ref[rows, :], g2_ref[...], wg_ref, wu_ref, wd_ref)
        y_ref[rows, :] = _rms_norm(x3, gf_ref[...])


def _resident(shape):
    return pl.BlockSpec(shape, lambda *_: (0,) * len(shape), pipeline_mode=pl.Buffered(1))


def _rope_tables(seq):
    rows = seq // GRID_W
    row_idx = np.repeat(np.arange(rows, dtype=np.float64), GRID_W)
    col_idx = np.tile(np.arange(GRID_W, dtype=np.float64), rows)
    inv_freq = ROPE_THETA ** (-np.arange(0, AXIS_DIM, 2, dtype=np.float64) / AXIS_DIM)
    ang_r = row_idx[:, None] * inv_freq[None, :]
    ang_c = col_idx[:, None] * inv_freq[None, :]
    cos_h = np.concatenate([np.cos(ang_r)] * 2 + [np.cos(ang_c)] * 2, axis=-1)
    sin_h = np.concatenate([-np.sin(ang_r), np.sin(ang_r), -np.sin(ang_c), np.sin(ang_c)], axis=-1)
    reps = LANES // HEAD_DIM
    return (np.tile(cos_h, (1, reps)).astype(np.float32), np.tile(sin_h, (1, reps)).astype(np.float32))


def _dft_cos_sin(n, rows, cols):
    j = np.arange(rows, dtype=np.int64)
    k = np.arange(cols, dtype=np.int64)
    ang = ((j[:, None] * k[None, :]) % n).astype(np.float64) * (2.0 * math.pi / n)
    return np.cos(ang).astype(np.float32), np.sin(ang).astype(np.float32)


def _cast_pad_cols_kernel(*refs):
    n = len(refs) // 2
    aligned = (D_FF // LANES) * LANES
    for wt_ref, o_ref in zip(refs[:n], refs[n:]):
        o_ref[:, :aligned] = wt_ref[:aligned, :].T.astype(BF16)
        tail = jnp.concatenate([wt_ref[aligned:, :], jnp.zeros((D_FF_PAD - D_FF, wt_ref.shape[1]), F32)], axis=0)
        o_ref[:, aligned:] = tail.T.astype(BF16)


def _cast_pad_rows_kernel(*refs):
    n = len(refs) // 2
    for w_ref, o_ref in zip(refs[:n], refs[n:]):
        o_ref[:D_FF, :] = w_ref[...].astype(BF16)
        o_ref[D_FF:, :] = jnp.zeros((D_FF_PAD - D_FF, o_ref.shape[1]), BF16)


def _cast_kernel(*refs):
    n = len(refs) // 2
    for w_ref, o_ref in zip(refs[:n], refs[n:]):
        o_ref[...] = w_ref[...].astype(BF16)


def _cast_ffn1_kernel(wgt_ref, wut_ref, wd_ref, win_ref, wg_out_ref, wu_out_ref, wd_out_ref, win_out_ref):
    _cast_pad_cols_kernel(wgt_ref, wut_ref, wg_out_ref, wu_out_ref)
    _cast_pad_rows_kernel(wd_ref, wd_out_ref)
    _cast_kernel(win_ref, win_out_ref)


def _row(v):
    return v.reshape(1, -1).astype(F32)


def kernel(x, ffn1_norm, ffn1_w_gate, ffn1_w_up, ffn1_w_down, mix_norm, w_in, fnet_w, fnet_b, q_norm,
           k_norm, w_out, ffn2_norm, ffn2_w_gate, ffn2_w_up, ffn2_w_down, final_norm):
    batch, seq, d = x.shape
    assert d == D_MODEL and seq % TOKEN_TILE == 0 and seq % ATTN_Q_TILE == 0 and seq % GRID_W == 0
    n_tok = batch * seq
    tiles_per_seq = seq // TOKEN_TILE
    x2d = x.reshape(n_tok, d)

    params = pltpu.CompilerParams(dimension_semantics=("arbitrary",), vmem_limit_bytes=VMEM_LIMIT_BYTES)
    tok = lambda width: pl.BlockSpec((TOKEN_TILE, width), lambda i: (i, 0))

    blk = d // WEIGHT_CAST_STEPS
    by_rows, by_cols = (lambda i: (i, 0)), (lambda i: (0, i))
    wg1, wu1, wd1, w_in_bf = pl.pallas_call(
        _cast_ffn1_kernel,
        grid=(WEIGHT_CAST_STEPS,),
        in_specs=[pl.BlockSpec((D_FF, blk), by_cols), pl.BlockSpec((D_FF, blk), by_cols),
                  pl.BlockSpec((D_FF, blk), by_cols), pl.BlockSpec((blk, IN_WIDTH), by_rows)],
        out_specs=[pl.BlockSpec((blk, D_FF_PAD), by_rows), pl.BlockSpec((blk, D_FF_PAD), by_rows),
                   pl.BlockSpec((D_FF_PAD, blk), by_cols), pl.BlockSpec((blk, IN_WIDTH), by_rows)],
        out_shape=[jax.ShapeDtypeStruct((d, D_FF_PAD), BF16), jax.ShapeDtypeStruct((d, D_FF_PAD), BF16),
                   jax.ShapeDtypeStruct((D_FF_PAD, d), BF16), jax.ShapeDtypeStruct((d, IN_WIDTH), BF16)],
        compiler_params=params,
        name="cast_ffn1",
    )(ffn1_w_gate.T, ffn1_w_up.T, ffn1_w_down, w_in)

    cos_t, sin_t = _rope_tables(seq)
    scale = math.log2(math.e) / math.sqrt(HEAD_DIM)
    gain_qk = jnp.concatenate([jnp.tile(q_norm.astype(F32) * scale, N_Q_HEADS),
                               jnp.tile(k_norm.astype(F32), N_KV_HEADS)]).reshape(1, QK_WIDTH)
    head_of_lane = np.arange(LANES) // HEAD_DIM
    seg_ones = jnp.asarray(head_of_lane[:, None] == head_of_lane[None, :], dtype=BF16)
    rope_spec = pl.BlockSpec((TOKEN_TILE, LANES), lambda i: (i % tiles_per_seq, 0))

    grp = lambda n, width: pl.BlockSpec((n, TOKEN_TILE, width), lambda i: (0, i, 0))
    vt_spec = pl.BlockSpec((None, N_KV_HEADS, HEAD_DIM, TOKEN_TILE),
                           lambda i: (i // tiles_per_seq, 0, 0, i % tiles_per_seq))
    x1, uf, q, kp, vt = pl.pallas_call(
        _ffn1_inproj_kernel,
        grid=(n_tok // TOKEN_TILE,),
        in_specs=[tok(d), _resident((1, d)),
                  _resident((d, D_FF_PAD)), _resident((d, D_FF_PAD)), _resident((D_FF_PAD, d)),
                  _resident((1, d)), _resident((d, IN_WIDTH)), _resident((1, QK_WIDTH)),
                  rope_spec, rope_spec, _resident((LANES, LANES))],
        out_specs=[tok(d), tok(FNET_WIDTH), grp(N_KV_HEADS, 2 * LANES), grp(2 * N_KV_HEADS, LANES), vt_spec],
        out_shape=[jax.ShapeDtypeStruct((n_tok, d), F32),
                   jax.ShapeDtypeStruct((n_tok, FNET_WIDTH), BF16),
                   jax.ShapeDtypeStruct((N_KV_HEADS, n_tok, 2 * LANES), BF16),
                   jax.ShapeDtypeStruct((2 * N_KV_HEADS, n_tok, LANES), BF16),
                   jax.ShapeDtypeStruct((batch, N_KV_HEADS, HEAD_DIM, seq), BF16)],
        compiler_params=params,
        name="ffn1_inproj",
    )(x2d, _row(ffn1_norm), wg1, wu1, wd1, _row(mix_norm), w_in_bf, gain_qk, cos_t, sin_t, seg_ones)

    half = seq // 2
    dft_rows = half + BF16_SUBLANES
    cos_s, sin_s = _dft_cos_sin(seq, dft_rows, half)
    idx = np.arange(half)
    rev = jnp.asarray((idx[:, None] + idx[None, :]) == half, dtype=BF16)
    cos_c, sin_c = _dft_cos_sin(FNET_GROUP_DIM, FNET_GROUP_DIM, FNET_GROUP_DIM)
    ortho = 1.0 / math.sqrt(seq * FNET_GROUP_DIM)
    fourier_steps = batch // FOURIER_BATCH
    wblk = d // fourier_steps
    assert d % fourier_steps == 0 and wblk % LANES == 0
    f_out, wg2, wu2, wd2, w_out_bf = pl.pallas_call(
        _fourier_kernel,
        grid=(fourier_steps,),
        in_specs=[pl.BlockSpec((FOURIER_BATCH, seq, FNET_WIDTH), lambda b: (b, 0, 0)),
                  _resident((dft_rows, half)), _resident((dft_rows, half)), _resident((half, half)),
                  _resident((FNET_GROUP_DIM, FNET_GROUP_DIM)), _resident((FNET_GROUP_DIM, FNET_GROUP_DIM)),
                  _resident((FNET_GROUPS, FNET_GROUP_DIM, FNET_GROUP_DIM)), _resident((1, FNET_WIDTH)),
                  pl.BlockSpec((D_FF, wblk), by_cols), pl.BlockSpec((D_FF, wblk), by_cols),
                  pl.BlockSpec((D_FF, wblk), by_cols), pl.BlockSpec((wblk, d), by_rows)],
        out_specs=[pl.BlockSpec((FOURIER_BATCH, seq, FNET_WIDTH), lambda b: (b, 0, 0)),
                   pl.BlockSpec((wblk, D_FF_PAD), by_rows), pl.BlockSpec((wblk, D_FF_PAD), by_rows),
                   pl.BlockSpec((D_FF_PAD, wblk), by_cols), pl.BlockSpec((wblk, d), by_rows)],
        out_shape=[jax.ShapeDtypeStruct((batch, seq, FNET_WIDTH), BF16),
                   jax.ShapeDtypeStruct((d, D_FF_PAD), BF16), jax.ShapeDtypeStruct((d, D_FF_PAD), BF16),
                   jax.ShapeDtypeStruct((D_FF_PAD, d), BF16), jax.ShapeDtypeStruct((d, d), BF16)],
        scratch_shapes=[pltpu.VMEM((FOURIER_BATCH, half, FNET_WIDTH), BF16),
                        pltpu.VMEM((FOURIER_BATCH, half, FNET_WIDTH), BF16)],
        compiler_params=params,
        name="fourier",
    )(uf.reshape(batch, seq, FNET_WIDTH), jnp.asarray(cos_s).astype(BF16), jnp.asarray(sin_s).astype(BF16), rev,
      cos_c * ortho, sin_c * ortho, fnet_w.astype(F32), fnet_b.reshape(1, FNET_WIDTH).astype(F32),
      ffn2_w_gate.T, ffn2_w_up.T, ffn2_w_down, w_out)

    a_out = pl.pallas_call(
        _attention_kernel,
        grid=(batch,),
        in_specs=[pl.BlockSpec((N_KV_HEADS, seq, 2 * LANES), lambda b: (0, b, 0)),
                  pl.BlockSpec((2 * N_KV_HEADS, seq, LANES), lambda b: (0, b, 0)),
                  pl.BlockSpec((None, N_KV_HEADS, HEAD_DIM, seq), lambda b: (b, 0, 0, 0))],
        out_specs=pl.BlockSpec((N_KV_HEADS, seq, 2 * LANES), lambda b: (0, b, 0)),
        out_shape=jax.ShapeDtypeStruct((N_KV_HEADS, n_tok, 2 * LANES), BF16),
        scratch_shapes=[pltpu.VMEM((2 * N_KV_HEADS, seq, 2 * ATTN_Q_TILE), F32),
                        pltpu.VMEM((2 * N_KV_HEADS, seq, 2 * ATTN_Q_TILE), BF16),
                        pltpu.VMEM((2 * N_KV_HEADS, 2, 1, 2 * ATTN_Q_TILE), F32),
                        pltpu.VMEM((2 * N_KV_HEADS, HEAD_DIM, 2 * ATTN_Q_TILE), F32)],
        compiler_params=params,
        name="attention",
    )(q, kp, vt)

    y = pl.pallas_call(
        _outproj_ffn2_kernel,
        grid=(n_tok // TOKEN_TILE,),
        in_specs=[tok(d), tok(FNET_WIDTH), grp(N_KV_HEADS, 2 * LANES),
                  _resident((d, d)), _resident((1, d)),
                  _resident((d, D_FF_PAD)), _resident((d, D_FF_PAD)), _resident((D_FF_PAD, d)),
                  _resident((1, d))],
        out_specs=tok(d),
        out_shape=jax.ShapeDtypeStruct((n_tok, d), F32),
        compiler_params=params,
        name="outproj_ffn2",
    )(x1, f_out.reshape(n_tok, FNET_WIDTH), a_out,
      w_out_bf, _row(ffn2_norm), wg2, wu2, wd2, _row(final_norm))
    return y.reshape(batch, seq, d)
```

```python
import math

import numpy as np
import jax
import jax.numpy as jnp
from jax import lax
from jax.experimental import pallas as pl
from jax.experimental.pallas import tpu as pltpu

D_MODEL = 1024
D_FF = 2752
FNET_WIDTH = 512
FNET_GROUPS = 4
FNET_GROUP_DIM = 128
HEAD_DIM = 64
N_Q_HEADS = 8
N_KV_HEADS = 2
ATTN_WIDTH = N_Q_HEADS * HEAD_DIM
KV_WIDTH = N_KV_HEADS * HEAD_DIM
QK_WIDTH = ATTN_WIDTH + KV_WIDTH
IN_WIDTH = FNET_WIDTH + ATTN_WIDTH + 2 * KV_WIDTH
GRID_W = 64
AXIS_DIM = HEAD_DIM // 2
ROPE_THETA = 10000.0
EPS = 1e-6

LANES = 128
BF16_SUBLANES = 16
MXU_DIM = 256
VMEM_LIMIT_BYTES = 60000 * 1024

D_FF_PAD = ((D_FF + MXU_DIM - 1) // MXU_DIM) * MXU_DIM

TOKEN_TILE = 1024
SUB_TILE = 256
ATTN_Q_TILE = 256
WEIGHT_CAST_STEPS = 8
FOURIER_BATCH = 2
ATTN_BATCH = 2

BF16 = jnp.bfloat16
F32 = jnp.float32


def _rms_norm(x, gain):
    return x * lax.rsqrt(jnp.mean(x * x, axis=-1, keepdims=True) + EPS) * gain


def _dot(a, b):
    return jnp.dot(a, b, preferred_element_type=F32)


def _swiglu_half_step(x, gain, wg_ref, wu_ref, wd_ref):
    xn = _rms_norm(x, gain).astype(BF16)
    g = _dot(xn, wg_ref[...])
    u = _dot(xn, wu_ref[...])
    a = (g * jax.nn.sigmoid(g) * u).astype(BF16)
    return x + 0.5 * _dot(a, wd_ref[...])


def _split_hi_lo(x):
    hi = x.astype(BF16)
    lo = (x - hi.astype(F32)).astype(BF16)
    return hi, lo


def _ffn1_inproj_kernel(x_ref, g1_ref, wg_ref, wu_ref, wd_ref, gm_ref, win_ref, gqk_ref,
                        cos_ref, sin_ref, seg_ref,
                        x1_ref, uf_ref, q_ref, kp_ref, vt_ref):
    seg = seg_ref[...]
    lane = lax.broadcasted_iota(jnp.int32, (1, LANES), 1)
    upper_half = (lane & (AXIS_DIM // 2)) != 0

    def gate_up(rows):
        xn = _rms_norm(x_ref[rows, :], g1_ref[...]).astype(BF16)
        g = _dot(xn, wg_ref[...])
        return (g * jax.nn.sigmoid(g) * _dot(xn, wu_ref[...])).astype(BF16)

    def down(rows, act):
        x1_ref[rows, :] = x_ref[rows, :] + 0.5 * _dot(act, wd_ref[...])

    def in_proj(rows):
        h = _rms_norm(x1_ref[rows, :], gm_ref[...]).astype(BF16)
        return _dot(h, win_ref[...])

    def heads(rows, u):
        uf_ref[rows, :] = u[:, :FNET_WIDTH].astype(BF16)
        vt = u[:, FNET_WIDTH + QK_WIDTH:].T.astype(BF16)
        for g in range(N_KV_HEADS):
            vt_ref[g, :, rows] = vt[g * HEAD_DIM:(g + 1) * HEAD_DIM]

        cos_t = cos_ref[rows, :]
        sin_t = sin_ref[rows, :]
        for c in range(QK_WIDTH // LANES):
            lo_col = FNET_WIDTH + c * LANES
            z = u[:, lo_col:lo_col + LANES]
            hi, lo = _split_hi_lo(z * z)
            mean_sq = (_dot(hi, seg) + _dot(lo, seg)) * (1.0 / HEAD_DIM)
            zn = z * lax.rsqrt(mean_sq + EPS) * gqk_ref[:, c * LANES:(c + 1) * LANES]
            partner = jnp.where(upper_half,
                                pltpu.roll(zn, AXIS_DIM // 2, axis=1),
                                pltpu.roll(zn, LANES - AXIS_DIM // 2, axis=1))
            out = zn * cos_t + partner * sin_t
            if c < ATTN_WIDTH // LANES:
                q_ref[c // 2, rows, (c % 2) * LANES:(c % 2 + 1) * LANES] = out.astype(BF16)
            else:
                swapped = pltpu.roll(out, HEAD_DIM, axis=1)
                first = lane < HEAD_DIM
                variants = (jnp.where(first, out, 0.0), jnp.where(first, 0.0, swapped),
                            jnp.where(first, swapped, 0.0), jnp.where(first, 0.0, out))
                for i, kv in enumerate(variants):
                    kp_ref[i, rows, :] = kv.astype(BF16)

    sub_tiles = [slice(r * SUB_TILE, (r + 1) * SUB_TILE) for r in range(TOKEN_TILE // SUB_TILE)]
    for slot in range(len(sub_tiles) + 1):
        cur = sub_tiles[slot] if slot < len(sub_tiles) else None
        prev = sub_tiles[slot - 1] if slot > 0 else None
        if cur is not None:
            act = gate_up(cur)
        if prev is not None:
            u_prev = in_proj(prev)
        if cur is not None:
            down(cur, act)
        if prev is not None:
            heads(prev, u_prev)


def _fourier_kernel(uf_ref, ch_ref, sh_ref, rev_ref, cc_ref, sc_ref, fw_ref, fb_ref,
                    wgt_ref, wut_ref, wd_ref, wo_ref,
                    out_ref, wg_out_ref, wu_out_ref, wd_out_ref, wo_out_ref, e1_ref, o2_ref):
    n_batch, n = uf_ref.shape[0], uf_ref.shape[1]
    h = n // 2
    rev = rev_ref[...]

    cc_hi, cc_lo = _split_hi_lo(cc_ref[...])
    sc_hi, sc_lo = _split_hi_lo(sc_ref[...])
    folded = []
    for g in range(FNET_GROUPS):
        w_hi, w_lo = _split_hi_lo(fw_ref[g])
        folded.append(((_dot(cc_hi, w_hi) + _dot(cc_hi, w_lo) + _dot(cc_lo, w_hi)).astype(BF16),
                       (_dot(sc_hi, w_hi) + _dot(sc_hi, w_lo) + _dot(sc_lo, w_hi)).astype(BF16)))

    def mirror(b):
        mirrored = _dot(rev, uf_ref[b, h:, :])
        lo32 = uf_ref[b, :h, :].astype(F32)
        return (lo32 + mirrored).astype(BF16), (lo32 - mirrored).astype(BF16)

    def channel_mix(b, even, odd):
        mid_rows = []
        for g, (wc, ws) in enumerate(folded):
            cols = slice(g * FNET_GROUP_DIM, (g + 1) * FNET_GROUP_DIM)
            e1_ref[b, :, cols] = _dot(even[:, cols], wc).astype(BF16)
            o2_ref[b, :, cols] = _dot(odd[:, cols], ws).astype(BF16)
            mid_rows.append(_dot(uf_ref[b, h:h + BF16_SUBLANES, cols], wc)[0:1])
        return jnp.concatenate(mid_rows, axis=-1)

    def position_dft(b, mid):
        a = _dot(ch_ref[...], e1_ref[b])
        s = _dot(sh_ref[...], o2_ref[b])
        j = lax.broadcasted_iota(jnp.int32, (a.shape[0], 1), 0)
        a = a + jnp.where((j & 1) == 0, 1.0, -1.0) * mid + fb_ref[...]
        out_ref[b, :h, :] = (a[:h] - s[:h]).astype(BF16)
        return (a + s).astype(BF16)

    def mirror_out(b, z):
        out_ref[b, h:, :] = _dot(rev, z[:h]).astype(BF16)
        out_ref[b, h:h + 1, :] = z[h:h + 1]

    halves = [mirror(b) for b in range(n_batch)]
    mids = [channel_mix(b, *halves[b]) for b in range(n_batch)]
    zs = [position_dft(b, mids[b]) for b in range(n_batch)]
    for b in range(n_batch):
        mirror_out(b, zs[b])

    _cast_pad_cols_kernel(wgt_ref, wut_ref, wg_out_ref, wu_out_ref)
    _cast_pad_rows_kernel(wd_ref, wd_out_ref)
    _cast_kernel(wo_ref, wo_out_ref)


def _attention_kernel(q_ref, kp_ref, vt_ref, o_ref, s_ref, p_ref, m_ref, att_ref):
    seq = vt_ref.shape[-1]
    tq = ATTN_Q_TILE
    tiles_per_seq = seq // tq
    n_tiles = vt_ref.shape[0] * tiles_per_seq
    n_streams = 2 * N_KV_HEADS
    ones_rows = jnp.ones((BF16_SUBLANES, seq), BF16)

    def q_rows(t):
        start = t * tq
        if not isinstance(start, int):
            start = pl.multiple_of(start, tq)
        return pl.ds(start, tq)

    def keys_of(t):
        start = (t // tiles_per_seq) * seq
        if not isinstance(start, int):
            start = pl.multiple_of(start, seq)
        return pl.ds(start, seq)

    def qk(t, r):
        g = r // 2
        qblk = q_ref[g, q_rows(t), :]
        qg = jnp.concatenate([qblk[:, :LANES], qblk[:, LANES:]], axis=0)
        s = lax.dot_general(kp_ref[r, keys_of(t), :], qg, (((1,), (1,)), ((), ())), preferred_element_type=F32)
        s_ref[r] = s
        m_ref[r, t % 2] = jnp.max(s, axis=0, keepdims=True)

    def ex(t, r):
        p_ref[r] = jnp.exp2(s_ref[r] - m_ref[r, t % 2]).astype(BF16)

    def pv(t, r):
        v1t = jnp.concatenate([vt_ref[t // tiles_per_seq, r // 2], ones_rows], axis=0)
        o = _dot(v1t, p_ref[r])
        att_ref[r] = o[:HEAD_DIM] * (1.0 / o[HEAD_DIM:HEAD_DIM + 1])

    def emit(t, g):
        for c in range(2):
            pair = jnp.concatenate([att_ref[2 * g + h, :, c * tq:(c + 1) * tq] for h in range(2)], axis=0)
            o_ref[g, q_rows(t), c * LANES:(c + 1) * LANES] = pair.T.astype(BF16)

    def tile(t, first=False, last=False):
        for r in range(n_streams):
            ex(t, r)
            if not last:
                qk(t + 1, r)
            if r > 0:
                pv(t, r - 1)
                if r % 2 == 0:
                    emit(t, r // 2 - 1)
            elif not first:
                pv(t - 1, n_streams - 1)
                emit(t - 1, N_KV_HEADS - 1)

    def loop_body(t, carry):
        tile(t)
        return carry

    for r in range(n_streams):
        qk(0, r)
    tile(0, first=True)
    lax.fori_loop(1, n_tiles - 1, loop_body, 0)
    tile(n_tiles - 1, last=True)
    pv(n_tiles - 1, n_streams - 1)
    emit(n_tiles - 1, N_KV_HEADS - 1)


def _outproj_ffn2_kernel(x1_ref, f_ref, a_ref, wo_ref, g2_ref, wg_ref, wu_ref, wd_ref, gf_ref, y_ref):
    sub_tiles = [slice(r * SUB_TILE, (r + 1) * SUB_TILE) for r in range(TOKEN_TILE // SUB_TILE)]
    for rows in sub_tiles:
        x2 = x1_ref[rows, :] + _dot(f_ref[rows, :], wo_ref[:FNET_WIDTH, :])
        for g in range(N_KV_HEADS):
            w_rows = slice(FNET_WIDTH + g * 2 * LANES, FNET_WIDTH + (g + 1) * 2 * LANES)
            x2 = x2 + _dot(a_ref[g, rows, :], wo_ref[w_rows, :])
        y_ref[rows, :] = x2
    for rows in sub_tiles:
        x3 = _swiglu_half_step(y_ref[rows, :], g2_ref[...], wg_ref, wu_ref, wd_ref)
        y_ref[rows, :] = _rms_norm(x3, gf_ref[...])


def _resident(shape):
    return pl.BlockSpec(shape, lambda *_: (0,) * len(shape), pipeline_mode=pl.Buffered(1))


def _rope_tables(seq):
    rows = seq // GRID_W
    row_idx = np.repeat(np.arange(rows, dtype=np.float64), GRID_W)
    col_idx = np.tile(np.arange(GRID_W, dtype=np.float64), rows)
    inv_freq = ROPE_THETA ** (-np.arange(0, AXIS_DIM, 2, dtype=np.float64) / AXIS_DIM)
    ang_r = row_idx[:, None] * inv_freq[None, :]
    ang_c = col_idx[:, None] * inv_freq[None, :]
    cos_h = np.concatenate([np.cos(ang_r)] * 2 + [np.cos(ang_c)] * 2, axis=-1)
    sin_h = np.concatenate([-np.sin(ang_r), np.sin(ang_r), -np.sin(ang_c), np.sin(ang_c)], axis=-1)
    reps = LANES // HEAD_DIM
    return (np.tile(cos_h, (1, reps)).astype(np.float32), np.tile(sin_h, (1, reps)).astype(np.float32))


def _dft_cos_sin(n, rows, cols):
    j = np.arange(rows, dtype=np.int64)
    k = np.arange(cols, dtype=np.int64)
    ang = ((j[:, None] * k[None, :]) % n).astype(np.float64) * (2.0 * math.pi / n)
    return np.cos(ang).astype(np.float32), np.sin(ang).astype(np.float32)


def _cast_pad_cols_kernel(*refs):
    n = len(refs) // 2
    aligned = (D_FF // LANES) * LANES
    for wt_ref, o_ref in zip(refs[:n], refs[n:]):
        o_ref[:, :aligned] = wt_ref[:aligned, :].T.astype(BF16)
        tail = jnp.concatenate([wt_ref[aligned:, :], jnp.zeros((D_FF_PAD - D_FF, wt_ref.shape[1]), F32)], axis=0)
        o_ref[:, aligned:] = tail.T.astype(BF16)


def _cast_pad_rows_kernel(*refs):
    n = len(refs) // 2
    for w_ref, o_ref in zip(refs[:n], refs[n:]):
        o_ref[:D_FF, :] = w_ref[...].astype(BF16)
        o_ref[D_FF:, :] = jnp.zeros((D_FF_PAD - D_FF, o_ref.shape[1]), BF16)


def _cast_kernel(*refs):
    n = len(refs) // 2
    for w_ref, o_ref in zip(refs[:n], refs[n:]):
        o_ref[...] = w_ref[...].astype(BF16)


def _cast_ffn1_kernel(wgt_ref, wut_ref, wd_ref, win_ref, wg_out_ref, wu_out_ref, wd_out_ref, win_out_ref):
    _cast_pad_cols_kernel(wgt_ref, wut_ref, wg_out_ref, wu_out_ref)
    _cast_pad_rows_kernel(wd_ref, wd_out_ref)
    _cast_kernel(win_ref, win_out_ref)


def _row(v):
    return v.reshape(1, -1).astype(F32)


def kernel(x, ffn1_norm, ffn1_w_gate, ffn1_w_up, ffn1_w_down, mix_norm, w_in, fnet_w, fnet_b, q_norm,
           k_norm, w_out, ffn2_norm, ffn2_w_gate, ffn2_w_up, ffn2_w_down, final_norm):
    batch, seq, d = x.shape
    assert d == D_MODEL and seq % TOKEN_TILE == 0 and seq % ATTN_Q_TILE == 0 and seq % GRID_W == 0
    n_tok = batch * seq
    tiles_per_seq = seq // TOKEN_TILE
    x2d = x.reshape(n_tok, d)

    params = pltpu.CompilerParams(dimension_semantics=("arbitrary",), vmem_limit_bytes=VMEM_LIMIT_BYTES)
    tok = lambda width: pl.BlockSpec((TOKEN_TILE, width), lambda i: (i, 0))

    blk = d // WEIGHT_CAST_STEPS
    by_rows, by_cols = (lambda i: (i, 0)), (lambda i: (0, i))
    wg1, wu1, wd1, w_in_bf = pl.pallas_call(
        _cast_ffn1_kernel,
        grid=(WEIGHT_CAST_STEPS,),
        in_specs=[pl.BlockSpec((D_FF, blk), by_cols), pl.BlockSpec((D_FF, blk), by_cols),
                  pl.BlockSpec((D_FF, blk), by_cols), pl.BlockSpec((blk, IN_WIDTH), by_rows)],
        out_specs=[pl.BlockSpec((blk, D_FF_PAD), by_rows), pl.BlockSpec((blk, D_FF_PAD), by_rows),
                   pl.BlockSpec((D_FF_PAD, blk), by_cols), pl.BlockSpec((blk, IN_WIDTH), by_rows)],
        out_shape=[jax.ShapeDtypeStruct((d, D_FF_PAD), BF16), jax.ShapeDtypeStruct((d, D_FF_PAD), BF16),
                   jax.ShapeDtypeStruct((D_FF_PAD, d), BF16), jax.ShapeDtypeStruct((d, IN_WIDTH), BF16)],
        compiler_params=params,
        name="cast_ffn1",
    )(ffn1_w_gate.T, ffn1_w_up.T, ffn1_w_down, w_in)

    cos_t, sin_t = _rope_tables(seq)
    scale = math.log2(math.e) / math.sqrt(HEAD_DIM)
    gain_qk = jnp.concatenate([jnp.tile(q_norm.astype(F32) * scale, N_Q_HEADS),
                               jnp.tile(k_norm.astype(F32), N_KV_HEADS)]).reshape(1, QK_WIDTH)
    head_of_lane = np.arange(LANES) // HEAD_DIM
    seg_ones = jnp.asarray(head_of_lane[:, None] == head_of_lane[None, :], dtype=BF16)
    rope_spec = pl.BlockSpec((TOKEN_TILE, LANES), lambda i: (i % tiles_per_seq, 0))

    grp = lambda n, width: pl.BlockSpec((n, TOKEN_TILE, width), lambda i: (0, i, 0))
    vt_spec = pl.BlockSpec((None, N_KV_HEADS, HEAD_DIM, TOKEN_TILE),
                           lambda i: (i // tiles_per_seq, 0, 0, i % tiles_per_seq))
    x1, uf, q, kp, vt = pl.pallas_call(
        _ffn1_inproj_kernel,
        grid=(n_tok // TOKEN_TILE,),
        in_specs=[tok(d), _resident((1, d)),
                  _resident((d, D_FF_PAD)), _resident((d, D_FF_PAD)), _resident((D_FF_PAD, d)),
                  _resident((1, d)), _resident((d, IN_WIDTH)), _resident((1, QK_WIDTH)),
                  rope_spec, rope_spec, _resident((LANES, LANES))],
        out_specs=[tok(d), tok(FNET_WIDTH), grp(N_KV_HEADS, 2 * LANES), grp(2 * N_KV_HEADS, LANES), vt_spec],
        out_shape=[jax.ShapeDtypeStruct((n_tok, d), F32),
                   jax.ShapeDtypeStruct((n_tok, FNET_WIDTH), BF16),
                   jax.ShapeDtypeStruct((N_KV_HEADS, n_tok, 2 * LANES), BF16),
                   jax.ShapeDtypeStruct((2 * N_KV_HEADS, n_tok, LANES), BF16),
                   jax.ShapeDtypeStruct((batch, N_KV_HEADS, HEAD_DIM, seq), BF16)],
        compiler_params=params,
        name="ffn1_inproj",
    )(x2d, _row(ffn1_norm), wg1, wu1, wd1, _row(mix_norm), w_in_bf, gain_qk, cos_t, sin_t, seg_ones)

    half = seq // 2
    dft_rows = half + BF16_SUBLANES
    cos_s, sin_s = _dft_cos_sin(seq, dft_rows, half)
    idx = np.arange(half)
    rev = jnp.asarray((idx[:, None] + idx[None, :]) == half, dtype=BF16)
    cos_c, sin_c = _dft_cos_sin(FNET_GROUP_DIM, FNET_GROUP_DIM, FNET_GROUP_DIM)
    ortho = 1.0 / math.sqrt(seq * FNET_GROUP_DIM)
    fourier_steps = batch // FOURIER_BATCH
    wblk = d // fourier_steps
    assert d % fourier_steps == 0 and wblk % LANES == 0
    f_out, wg2, wu2, wd2, w_out_bf = pl.pallas_call(
        _fourier_kernel,
        grid=(fourier_steps,),
        in_specs=[pl.BlockSpec((FOURIER_BATCH, seq, FNET_WIDTH), lambda b: (b, 0, 0)),
                  _resident((dft_rows, half)), _resident((dft_rows, half)), _resident((half, half)),
                  _resident((FNET_GROUP_DIM, FNET_GROUP_DIM)), _resident((FNET_GROUP_DIM, FNET_GROUP_DIM)),
                  _resident((FNET_GROUPS, FNET_GROUP_DIM, FNET_GROUP_DIM)), _resident((1, FNET_WIDTH)),
                  pl.BlockSpec((D_FF, wblk), by_cols), pl.BlockSpec((D_FF, wblk), by_cols),
                  pl.BlockSpec((D_FF, wblk), by_cols), pl.BlockSpec((wblk, d), by_rows)],
        out_specs=[pl.BlockSpec((FOURIER_BATCH, seq, FNET_WIDTH), lambda b: (b, 0, 0)),
                   pl.BlockSpec((wblk, D_FF_PAD), by_rows), pl.BlockSpec((wblk, D_FF_PAD), by_rows),
                   pl.BlockSpec((D_FF_PAD, wblk), by_cols), pl.BlockSpec((wblk, d), by_rows)],
        out_shape=[jax.ShapeDtypeStruct((batch, seq, FNET_WIDTH), BF16),
                   jax.ShapeDtypeStruct((d, D_FF_PAD), BF16), jax.ShapeDtypeStruct((d, D_FF_PAD), BF16),
                   jax.ShapeDtypeStruct((D_FF_PAD, d), BF16), jax.ShapeDtypeStruct((d, d), BF16)],
        scratch_shapes=[pltpu.VMEM((FOURIER_BATCH, half, FNET_WIDTH), BF16),
                        pltpu.VMEM((FOURIER_BATCH, half, FNET_WIDTH), BF16)],
        compiler_params=params,
        name="fourier",
    )(uf.reshape(batch, seq, FNET_WIDTH), jnp.asarray(cos_s).astype(BF16), jnp.asarray(sin_s).astype(BF16), rev,
      cos_c * ortho, sin_c * ortho, fnet_w.astype(F32), fnet_b.reshape(1, FNET_WIDTH).astype(F32),
      ffn2_w_gate.T, ffn2_w_up.T, ffn2_w_down, w_out)

    a_out = pl.pallas_call(
        _attention_kernel,
        grid=(batch // ATTN_BATCH,),
        in_specs=[pl.BlockSpec((N_KV_HEADS, ATTN_BATCH * seq, 2 * LANES), lambda b: (0, b, 0)),
                  pl.BlockSpec((2 * N_KV_HEADS, ATTN_BATCH * seq, LANES), lambda b: (0, b, 0)),
                  pl.BlockSpec((ATTN_BATCH, N_KV_HEADS, HEAD_DIM, seq), lambda b: (b, 0, 0, 0))],
        out_specs=pl.BlockSpec((N_KV_HEADS, ATTN_BATCH * seq, 2 * LANES), lambda b: (0, b, 0)),
        out_shape=jax.ShapeDtypeStruct((N_KV_HEADS, n_tok, 2 * LANES), BF16),
        scratch_shapes=[pltpu.VMEM((2 * N_KV_HEADS, seq, 2 * ATTN_Q_TILE), F32),
                        pltpu.VMEM((2 * N_KV_HEADS, seq, 2 * ATTN_Q_TILE), BF16),
                        pltpu.VMEM((2 * N_KV_HEADS, 2, 1, 2 * ATTN_Q_TILE), F32),
                        pltpu.VMEM((2 * N_KV_HEADS, HEAD_DIM, 2 * ATTN_Q_TILE), F32)],
        compiler_params=params,
        name="attention",
    )(q, kp, vt)

    y = pl.pallas_call(
        _outproj_ffn2_kernel,
        grid=(n_tok // TOKEN_TILE,),
        in_specs=[tok(d), tok(FNET_WIDTH), grp(N_KV_HEADS, 2 * LANES),
                  _resident((d, d)), _resident((1, d)),
                  _resident((d, D_FF_PAD)), _resident((d, D_FF_PAD)), _resident((D_FF_PAD, d)),
                  _resident((1, d))],
        out_specs=tok(d),
        out_shape=jax.ShapeDtypeStruct((n_tok, d), F32),
        compiler_params=params,
        name="outproj_ffn2",
    )(x1, f_out.reshape(n_tok, FNET_WIDTH), a_out,
      w_out_bf, _row(ffn2_norm), wg2, wu2, wd2, _row(final_norm))
    return y.reshape(batch, seq, d)
```
